```python
import math
import jax, jax.numpy as jnp
from jax import lax
import numpy as np

D_MODEL = 1024
BATCH = 4
SEQ = 4096
DEPTH = 4
DEC_BATCH = 32
DEC_SEQ = 4
PAST_LEN = 8192
PAGE_SIZE = 128

HEAD_DIM = 64
H_MIX = 4
W_MIX = H_MIX * HEAD_DIM
N_BRANCH = 4
GDN_CONV = 4
GDN_CHUNK = 64
NSA_CMP_LEN = 32
NSA_CMP_STRIDE = 16
NSA_SEL_BLOCK = 64
NSA_TOPN = 16
NSA_WINDOW = 512
NSA_FORCE = 1e4
MOBA_BLOCK = 256
MOBA_TOPK = 3
D_FF = 4 * D_MODEL
Q_BLOCK = 128
MOBA_Q_BLOCK = 32
ALPHA_DN = (2 * DEPTH) ** 0.25
BETA_DN = (8 * DEPTH) ** -0.25
LN_EPS = 1e-5
NEG = -1e30

SPLIT_NAMES = ('gdn_qkv', 'gdn_z', 'gdn_b', 'gdn_a',
               'fox_q', 'fox_k', 'fox_v', 'fox_f',
               'nsa_q', 'nsa_ck', 'nsa_cv', 'nsa_sk', 'nsa_sv', 'nsa_wk', 'nsa_wv', 'nsa_g',
               'moba_q', 'moba_k', 'moba_v',
               'merge_g')
SPLIT_SIZES = (3 * W_MIX, W_MIX, H_MIX, H_MIX,
               W_MIX, W_MIX, W_MIX, H_MIX,
               W_MIX, HEAD_DIM, HEAD_DIM, HEAD_DIM, HEAD_DIM, HEAD_DIM, HEAD_DIM, 3 * H_MIX,
               W_MIX, W_MIX, W_MIX,
               N_BRANCH * D_MODEL)
N_IN = sum(SPLIT_SIZES)
STATE_NAMES = ('gdn_s', 'gdn_conv', 'fox_k', 'fox_v', 'fox_logf', 'nsa_ck', 'nsa_cv', 'nsa_sk', 'nsa_sv',
               'nsa_wk', 'nsa_wv', 'moba_k', 'moba_v')

kernel_name = 'hybrid_gdn_fox_nsa_moba_step'


def layer_norm(x, g, b):
    xf = x.astype(jnp.float32)
    mu = jnp.mean(xf, -1, keepdims=True)
    var = jnp.mean(jnp.square(xf - mu), -1, keepdims=True)
    return ((xf - mu) * lax.rsqrt(var + LN_EPS)).astype(x.dtype) * g + b


def rms_normalize(x):
    return x * lax.rsqrt(jnp.mean(jnp.square(x), -1, keepdims=True) + 1e-6)


def l2_normalize(x):
    x = x.astype(jnp.float32)
    return x * lax.rsqrt(jnp.sum(jnp.square(x), -1, keepdims=True) + 1e-6)


def masked_softmax(logits, mask):
    z = jnp.where(mask, logits.astype(jnp.float32), NEG)
    e = jnp.exp(z - jnp.max(z, -1, keepdims=True)) * mask
    return e / jnp.maximum(jnp.sum(e, -1, keepdims=True), 1e-30)


def alibi_slopes():
    n = 2 * H_MIX
    s = 2.0 ** (-8.0 * jnp.arange(1, n + 1, dtype=jnp.float32) / n)
    return s[0::2], s[1::2]


def split_proj(h):
    out, off = {}, 0
    for name, size in zip(SPLIT_NAMES, SPLIT_SIZES):
        out[name] = h[..., off:off + size]
        off += size
    return out


def over_query_blocks(fn, qs, qb):
    B, T = qs[0].shape[:2]
    nb = T // qb
    blocks = tuple(jnp.moveaxis(a.reshape((B, nb, qb) + a.shape[2:]), 1, 0) for a in qs)
    pos = jnp.arange(T, dtype=jnp.int32).reshape(nb, qb)
    out = lax.map(lambda args: fn(*args), blocks + (pos,))
    out = jnp.moveaxis(out, 0, 1)
    return out.reshape((B, T) + out.shape[3:])


def key_blocks(k, size):
    L = k.shape[1]
    n = -(-L // size)
    k = jnp.pad(k, [(0, 0), (0, n * size - L)] + [(0, 0)] * (k.ndim - 2))
    return k.reshape((k.shape[0], n, size) + k.shape[2:])


def gather_pages(cache, page_table, layer):
    g = cache[page_table, layer]
    return g.reshape((g.shape[0], g.shape[1] * g.shape[2]) + g.shape[3:])


def short_conv(x, buf, w):
    xp = jnp.concatenate([buf, x], 1)
    T = x.shape[1]
    y = sum(xp[:, j:j + T] * w[j] for j in range(GDN_CONV))
    return jax.nn.silu(y), xp[:, xp.shape[1] - (GDN_CONV - 1):]


def gated_delta(q, k, v, g, beta, s0):
    B, T, H, DK = q.shape
    DV = v.shape[-1]
    C = math.gcd(T, GDN_CHUNK)
    N = T // C
    f32 = jnp.float32

    def chunked(a):
        a = a.astype(f32).reshape((B, N, C, H) + a.shape[3:])
        return jnp.moveaxis(jnp.moveaxis(a, 3, 2), 1, 0)

    qc, kc, vc, gc, bc = (chunked(a) for a in (q, k, v, g, beta))
    gam = jnp.cumsum(gc, -1)
    diff = gam[..., :, None] - gam[..., None, :]
    low_s = jnp.tril(jnp.ones((C, C), bool), -1)
    low = jnp.tril(jnp.ones((C, C), bool))
    dec_s = jnp.where(low_s, jnp.exp(jnp.where(low_s, diff, 0.0)), 0.0)
    dec = jnp.where(low, jnp.exp(jnp.where(low, diff, 0.0)), 0.0)
    a_mat = bc[..., :, None] * dec_s * jnp.einsum('nbhtk,nbhik->nbhti', kc, kc)
    gam_exp = jnp.exp(gam)
    rhs = jnp.concatenate([(bc * gam_exp)[..., None] * kc, bc[..., None] * vc], -1)
    sol = lax.linalg.triangular_solve(a_mat + jnp.eye(C, dtype=f32), rhs, left_side=True, lower=True,
                                      unit_diagonal=True)
    w_c, u_c = sol[..., :DK], sol[..., DK:]
    p_c = dec * jnp.einsum('nbhtk,nbhik->nbhti', qc, kc)
    qg = gam_exp[..., None] * qc
    g_last = gam[..., -1]
    kd = jnp.exp(g_last[..., None] - gam)[..., None] * kc

    def step(s, xs):
        w_, u_, p_, qg_, kd_, gl_ = xs
        u = u_ - jnp.einsum('bhtk,bhkv->bhtv', w_, s)
        o = jnp.einsum('bhtk,bhkv->bhtv', qg_, s) + jnp.einsum('bhti,bhiv->bhtv', p_, u)
        s = jnp.exp(gl_)[..., None, None] * s + jnp.einsum('bhik,bhiv->bhkv', kd_, u)
        return s, o

    s_fin, o = lax.scan(step, s0.astype(f32), (w_c, u_c, p_c, qg, kd, g_last))
    o = jnp.swapaxes(jnp.moveaxis(o, 0, 1), 2, 3).reshape(B, T, H, DV)
    return o, s_fin.astype(s0.dtype)


def gdn_branch(p, conv_buf, s0, lw):
    B, T, _ = p['gdn_qkv'].shape
    hs = (B, T, H_MIX, HEAD_DIM)
    qkv, new_buf = short_conv(p['gdn_qkv'], conv_buf, lw['gdn_conv_w'])
    q, k, v = jnp.split(qkv, 3, -1)
    q = l2_normalize(q.reshape(hs)) * HEAD_DIM ** -0.5
    k = l2_normalize(k.reshape(hs))
    beta = jax.nn.sigmoid(p['gdn_b'].astype(jnp.float32))
    decay = -jnp.exp(lw['gdn_a_log'].astype(jnp.float32)) * jax.nn.softplus(
        p['gdn_a'].astype(jnp.float32) + lw['gdn_dt_bias'])
    o, s_new = gated_delta(q, k, v.reshape(hs), decay, beta, s0)
    o = rms_normalize(o) * lw['gdn_norm_g'] * jax.nn.silu(p['gdn_z'].reshape(hs).astype(jnp.float32))
    return o.reshape(B, T, W_MIX).astype(p['gdn_qkv'].dtype), new_buf, s_new


def fox_log_forget(f_raw, f_bias):
    return jax.nn.log_sigmoid(f_raw.astype(jnp.float32) + f_bias)


def fox_attend(q, qpos, cq, k, v, ck):
    L = k.shape[1]
    s = jnp.einsum('bqhd,bkhd->bhqk', q, k).astype(jnp.float32) * HEAD_DIM ** -0.5
    s = s + (jnp.moveaxis(cq, 1, 2)[..., :, None] - jnp.moveaxis(ck, 1, 2)[..., None, :])
    mask = jnp.arange(L, dtype=jnp.int32)[None, :] <= qpos[:, None]
    pr = masked_softmax(s, mask)
    return jnp.einsum('bhqk,bkhd->bqhd', pr.astype(v.dtype), v)


def nsa_compress(rows, pos_emb, w1, w2):
    B, L, d = rows.shape
    n_cmp = (L - NSA_CMP_LEN) // NSA_CMP_STRIDE + 1
    starts = jnp.arange(n_cmp, dtype=jnp.int32) * NSA_CMP_STRIDE
    blk = rows[:, starts[:, None] + jnp.arange(NSA_CMP_LEN)[None, :]] + pos_emb
    out = jax.nn.silu(blk.reshape(B, n_cmp, NSA_CMP_LEN * d) @ w1) @ w2
    return out, starts + (NSA_CMP_LEN - 1)


def nsa_attend(q, qpos, gates, kc, vc, cmp_end, ks_blk, vs_blk, kw, vw, kwpos, slopes):
    B, Tq, H, _ = q.shape
    f32 = jnp.float32
    n_sel = ks_blk.shape[1]
    scale = HEAD_DIM ** -0.5
    sl = slopes[None, :, None, None]
    qp = qpos.astype(f32)
    s_c = jnp.einsum('bqhd,bnd->bhqn', q, kc).astype(f32) * scale - sl * (qp[:, None] - cmp_end[None, :].astype(f32))
    p_c = masked_softmax(s_c, cmp_end[None, :] <= qpos[:, None])
    o_c = jnp.einsum('bhqn,bnd->bqhd', p_c.astype(vc.dtype), vc)
    blk = jnp.arange(n_sel, dtype=jnp.int32)
    c_start = cmp_end - (NSA_CMP_LEN - 1)
    overlap = ((c_start[:, None] < (blk[None, :] + 1) * NSA_SEL_BLOCK)
               & (cmp_end[:, None] >= blk[None, :] * NSA_SEL_BLOCK)).astype(f32)
    imp = jnp.einsum('bhqn,ns->bqs', p_c, overlap)
    cur = (qpos // NSA_SEL_BLOCK)[:, None]
    valid = blk[None, :] <= cur
    forced = (blk[None, :] == 0) | (blk[None, :] == cur) | (blk[None, :] == cur - 1)
    score = jnp.where(valid, jnp.where(forced, NSA_FORCE, imp), NEG)
    n_top = min(NSA_TOPN, n_sel)
    top_s, top_i = lax.top_k(score, n_top)
    bi = jnp.arange(B)[:, None, None]
    k_g = ks_blk[bi, top_i].reshape(B, Tq, n_top * NSA_SEL_BLOCK, HEAD_DIM)
    v_g = vs_blk[bi, top_i].reshape(B, Tq, n_top * NSA_SEL_BLOCK, HEAD_DIM)
    pos_g = (top_i[..., None] * NSA_SEL_BLOCK + jnp.arange(NSA_SEL_BLOCK)).reshape(B, Tq, -1)
    m_g = jnp.repeat(top_s > 0.5 * NEG, NSA_SEL_BLOCK, axis=-1) & (pos_g <= qpos[None, :, None])
    s_g = jnp.einsum('bqhd,bqkd->bhqk', q, k_g).astype(f32) * scale - sl * (qp[None, None, :, None] - pos_g[:, None].astype(f32))
    p_g = masked_softmax(s_g, m_g[:, None])
    o_g = jnp.einsum('bhqk,bqkd->bqhd', p_g.astype(v_g.dtype), v_g)
    dist = qpos[:, None] - kwpos[None, :]
    s_w = jnp.einsum('bqhd,bkd->bhqk', q, kw).astype(f32) * scale - sl * dist.astype(f32)
    m_w = (dist >= 0) & (dist < NSA_WINDOW) & (kwpos[None, :] >= 0)
    p_w = masked_softmax(s_w, m_w)
    o_w = jnp.einsum('bhqk,bkd->bqhd', p_w.astype(vw.dtype), vw)
    out = gates[..., 0:1] * o_c + gates[..., 1:2] * o_g + gates[..., 2:3] * o_w
    return out.astype(q.dtype)


def moba_attend(q, qpos, k_blk, v_blk, k_mean, slopes):
    B, Tq, H, _ = q.shape
    f32 = jnp.float32
    n_blk = k_blk.shape[2]
    qh = jnp.swapaxes(q, 1, 2)
    cur = qpos // MOBA_BLOCK
    gs = jnp.einsum('bhqd,bhnd->bhqn', qh, k_mean).astype(f32)
    gs = jnp.where(jnp.arange(n_blk)[None, :] < cur[:, None], gs, NEG)
    n_top = min(MOBA_TOPK, n_blk)
    top_s, top_i = lax.top_k(gs, n_top)
    bi = jnp.arange(B)[:, None, None, None]
    hi = jnp.arange(H)[None, :, None, None]
    k_sel = k_blk[bi, hi, top_i].reshape(B, H, Tq, n_top * MOBA_BLOCK, HEAD_DIM)
    v_sel = v_blk[bi, hi, top_i].reshape(B, H, Tq, n_top * MOBA_BLOCK, HEAD_DIM)
    keys = jnp.concatenate([k_sel, k_blk[:, :, cur]], 3)
    vals = jnp.concatenate([v_sel, v_blk[:, :, cur]], 3)
    pos_sel = (top_i[..., None] * MOBA_BLOCK + jnp.arange(MOBA_BLOCK)).reshape(B, H, Tq, -1)
    pos_own = cur[:, None] * MOBA_BLOCK + jnp.arange(MOBA_BLOCK)[None, :]
    pos = jnp.concatenate([pos_sel, jnp.broadcast_to(pos_own, (B, H, Tq, MOBA_BLOCK))], -1)
    mask = jnp.concatenate([jnp.repeat(top_s > 0.5 * NEG, MOBA_BLOCK, axis=-1),
                            jnp.broadcast_to(pos_own <= qpos[:, None], (B, H, Tq, MOBA_BLOCK))], -1)
    s = jnp.einsum('bhqd,bhqkd->bhqk', qh, keys).astype(f32) * HEAD_DIM ** -0.5
    s = s - slopes[None, :, None, None] * (qpos[None, None, :, None] - pos).astype(f32)
    pr = masked_softmax(s, mask)
    return jnp.einsum('bhqk,bhqkd->bqhd', pr.astype(vals.dtype), vals)


def prompt_mixers(p, lw, sl_nsa, sl_moba):
    B, T, _ = p['fox_q'].shape
    hs = (B, T, H_MIX, HEAD_DIM)
    dt = p['fox_q'].dtype
    new = {}
    conv0 = jnp.zeros((B, GDN_CONV - 1, 3 * W_MIX), dt)
    s0 = jnp.zeros((B, H_MIX, HEAD_DIM, HEAD_DIM), dt)
    o_a, new['gdn_conv'], new['gdn_s'] = gdn_branch(p, conv0, s0, lw)
    q, k, v = (p[n].reshape(hs) for n in ('fox_q', 'fox_k', 'fox_v'))
    logf = fox_log_forget(p['fox_f'], lw['fox_f_bias'])
    c = jnp.cumsum(logf, 1)
    o_b = over_query_blocks(lambda qb, cb, pos: fox_attend(qb, pos, cb, k, v, c), (q, c), Q_BLOCK)
    new['fox_k'], new['fox_v'], new['fox_logf'] = k, v, logf
    qn = p['nsa_q'].reshape(hs)
    gates = jax.nn.sigmoid(p['nsa_g'].astype(jnp.float32)).reshape(B, T, H_MIX, 3)
    kc, ends = nsa_compress(p['nsa_ck'], lw['nsa_cmp_pos_k'], lw['nsa_cmp_k_w1'], lw['nsa_cmp_k_w2'])
    vc, _ = nsa_compress(p['nsa_cv'], lw['nsa_cmp_pos_v'], lw['nsa_cmp_v_w1'], lw['nsa_cmp_v_w2'])
    ks_blk = key_blocks(p['nsa_sk'], NSA_SEL_BLOCK)
    vs_blk = key_blocks(p['nsa_sv'], NSA_SEL_BLOCK)
    kw_pad = jnp.pad(p['nsa_wk'], [(0, 0), (NSA_WINDOW, 0), (0, 0)])
    vw_pad = jnp.pad(p['nsa_wv'], [(0, 0), (NSA_WINDOW, 0), (0, 0)])
    span = NSA_WINDOW + Q_BLOCK

    def nsa_block(qb, gb, pos):
        start = pos[0]
        kw = lax.dynamic_slice_in_dim(kw_pad, start, span, 1)
        vw = lax.dynamic_slice_in_dim(vw_pad, start, span, 1)
        kwpos = start - NSA_WINDOW + jnp.arange(span, dtype=jnp.int32)
        return nsa_attend(qb, pos, gb, kc, vc, ends, ks_blk, vs_blk, kw, vw, kwpos, sl_nsa)

    o_c = over_query_blocks(nsa_block, (qn, gates), Q_BLOCK)
    keep = min(NSA_WINDOW, T)
    for n in ('nsa_ck', 'nsa_cv', 'nsa_sk', 'nsa_sv'):
        new[n] = p[n]
    new['nsa_wk'], new['nsa_wv'] = p['nsa_wk'][:, T - keep:], p['nsa_wv'][:, T - keep:]
    qm, km, vm = (p[n].reshape(hs) for n in ('moba_q', 'moba_k', 'moba_v'))
    k_blk = jnp.moveaxis(key_blocks(km, MOBA_BLOCK), 3, 1)
    v_blk = jnp.moveaxis(key_blocks(vm, MOBA_BLOCK), 3, 1)
    k_mean = jnp.mean(k_blk, 3)
    o_d = over_query_blocks(lambda qb, pos: moba_attend(qb, pos, k_blk, v_blk, k_mean, sl_moba), (qm,), MOBA_Q_BLOCK)
    new['moba_k'], new['moba_v'] = km, vm
    branches = jnp.stack([o_a, o_b.reshape(B, T, W_MIX), o_c.reshape(B, T, W_MIX), o_d.reshape(B, T, W_MIX)], 2)
    return branches, new


def sample_mixers(p, past, lw, sl_nsa, sl_moba):
    B, D, _ = p['fox_q'].shape
    P = past['fox_k'].shape[1]
    qpos = P + jnp.arange(D, dtype=jnp.int32)
    hs = (B, D, H_MIX, HEAD_DIM)
    new = {}
    cat = lambda n: jnp.concatenate([past[n], p[n]], 1)
    o_a, new['gdn_conv'], new['gdn_s'] = gdn_branch(p, past['gdn_conv'], past['gdn_s'], lw)
    q, k, v = (p[n].reshape(hs) for n in ('fox_q', 'fox_k', 'fox_v'))
    logf = fox_log_forget(p['fox_f'], lw['fox_f_bias'])
    c = jnp.cumsum(jnp.concatenate([past['fox_logf'].astype(jnp.float32), logf], 1), 1)
    o_b = fox_attend(q, qpos, c[:, P:], jnp.concatenate([past['fox_k'], k], 1),
                     jnp.concatenate([past['fox_v'], v], 1), c)
    new['fox_k'], new['fox_v'], new['fox_logf'] = k, v, logf
    qn = p['nsa_q'].reshape(hs)
    gates = jax.nn.sigmoid(p['nsa_g'].astype(jnp.float32)).reshape(B, D, H_MIX, 3)
    kc, ends = nsa_compress(cat('nsa_ck'), lw['nsa_cmp_pos_k'], lw['nsa_cmp_k_w1'], lw['nsa_cmp_k_w2'])
    vc, _ = nsa_compress(cat('nsa_cv'), lw['nsa_cmp_pos_v'], lw['nsa_cmp_v_w1'], lw['nsa_cmp_v_w2'])
    kw, vw = cat('nsa_wk'), cat('nsa_wv')
    n_buf = past['nsa_wk'].shape[1]
    kwpos = jnp.concatenate([P - n_buf + jnp.arange(n_buf, dtype=jnp.int32), qpos])
    o_c = nsa_attend(qn, qpos, gates, kc, vc, ends, key_blocks(cat('nsa_sk'), NSA_SEL_BLOCK),
                     key_blocks(cat('nsa_sv'), NSA_SEL_BLOCK), kw, vw, kwpos, sl_nsa)
    keep = min(NSA_WINDOW, kw.shape[1])
    for n in ('nsa_ck', 'nsa_cv', 'nsa_sk', 'nsa_sv'):
        new[n] = p[n]
    new['nsa_wk'], new['nsa_wv'] = kw[:, kw.shape[1] - keep:], vw[:, vw.shape[1] - keep:]
    qm, km, vm = (p[n].reshape(hs) for n in ('moba_q', 'moba_k', 'moba_v'))
    k_blk = jnp.moveaxis(key_blocks(jnp.concatenate([past['moba_k'], km], 1), MOBA_BLOCK), 3, 1)
    v_blk = jnp.moveaxis(key_blocks(jnp.concatenate([past['moba_v'], vm], 1), MOBA_BLOCK), 3, 1)
    o_d = moba_attend(qm, qpos, k_blk, v_blk, jnp.mean(k_blk, 3), sl_moba)
    new['moba_k'], new['moba_v'] = km, vm
    branches = jnp.stack([o_a, o_b.reshape(B, D, W_MIX), o_c.reshape(B, D, W_MIX), o_d.reshape(B, D, W_MIX)], 2)
    return branches, new


def finish_layer(x, branches, gate_raw, lw):
    B, T, _ = x.shape
    up = jnp.einsum('btnc,ncd->btnd', branches, lw['w_branch'])
    gate = jax.nn.sigmoid(gate_raw.reshape(B, T, N_BRANCH, D_MODEL))
    mixed = jnp.sum(gate * up, 2) @ lw['w_out']
    x = layer_norm(ALPHA_DN * x + mixed, lw['ln1_g'], lw['ln1_b'])
    hmid = jnp.square(jax.nn.relu(x @ lw['w_up']))
    return layer_norm(ALPHA_DN * x + hmid @ lw['w_down'], lw['ln2_g'], lw['ln2_b'])


def setup_inputs(seed: int = 0) -> dict:
    key = jax.random.key(seed)
    keys = iter(jax.random.split(key, 64))
    f32 = jnp.float32
    n_pages = PAST_LEN // PAGE_SIZE
    n_pool = (DEC_BATCH * n_pages * 5) // 4
    w_buf = min(NSA_WINDOW, PAST_LEN)
    cl = NSA_CMP_LEN * HEAD_DIM

    def nrm(shape, scale=1.0):
        return jax.random.normal(next(keys), shape, f32) * scale

    def unif(shape, lo, hi):
        return jax.random.uniform(next(keys), shape, f32, lo, hi)

    page_table = jax.random.permutation(next(keys), n_pool)[:DEC_BATCH * n_pages].reshape(
        DEC_BATCH, n_pages).astype(jnp.int32)
    dt = jnp.exp(unif((DEPTH, H_MIX), math.log(1e-3), math.log(1e-1)))
    return {
        'x_prompt': nrm((BATCH, SEQ, D_MODEL)),
        'x_sample': nrm((DEC_BATCH, DEC_SEQ, D_MODEL)),
        'state_gdn_s': nrm((DEC_BATCH, DEPTH, H_MIX, HEAD_DIM, HEAD_DIM), 0.3),
        'state_gdn_conv': nrm((DEC_BATCH, DEPTH, GDN_CONV - 1, 3 * W_MIX)),
        'cache_fox_k': nrm((n_pool, DEPTH, PAGE_SIZE, H_MIX, HEAD_DIM)),
        'cache_fox_v': nrm((n_pool, DEPTH, PAGE_SIZE, H_MIX, HEAD_DIM)),
        'cache_fox_logf': jax.nn.log_sigmoid(nrm((n_pool, DEPTH, PAGE_SIZE, H_MIX)) + 3.0),
        'cache_nsa_ck': nrm((n_pool, DEPTH, PAGE_SIZE, HEAD_DIM)),
        'cache_nsa_cv': nrm((n_pool, DEPTH, PAGE_SIZE, HEAD_DIM)),
        'cache_nsa_sk': nrm((n_pool, DEPTH, PAGE_SIZE, HEAD_DIM)),
        'cache_nsa_sv': nrm((n_pool, DEPTH, PAGE_SIZE, HEAD_DIM)),
        'state_nsa_wk': nrm((DEC_BATCH, DEPTH, w_buf, HEAD_DIM)),
        'state_nsa_wv': nrm((DEC_BATCH, DEPTH, w_buf, HEAD_DIM)),
        'cache_moba_k': nrm((n_pool, DEPTH, PAGE_SIZE, H_MIX, HEAD_DIM)),
        'cache_moba_v': nrm((n_pool, DEPTH, PAGE_SIZE, H_MIX, HEAD_DIM)),
        'page_table': page_table,
        'w_in': nrm((DEPTH, D_MODEL, N_IN), D_MODEL ** -0.5),
        'gdn_conv_w': nrm((DEPTH, GDN_CONV, 3 * W_MIX), GDN_CONV ** -0.5),
        'gdn_a_log': jnp.log(unif((DEPTH, H_MIX), 1.0, 16.0)),
        'gdn_dt_bias': dt + jnp.log(-jnp.expm1(-dt)),
        'gdn_norm_g': 1.0 + nrm((DEPTH, HEAD_DIM), 0.02),
        'fox_f_bias': unif((DEPTH, H_MIX), 2.0, 4.0),
        'nsa_cmp_pos_k': nrm((DEPTH, NSA_CMP_LEN, HEAD_DIM), 0.02),
        'nsa_cmp_k_w1': nrm((DEPTH, cl, HEAD_DIM), cl ** -0.5),
        'nsa_cmp_k_w2': nrm((DEPTH, HEAD_DIM, HEAD_DIM), HEAD_DIM ** -0.5),
        'nsa_cmp_pos_v': nrm((DEPTH, NSA_CMP_LEN, HEAD_DIM), 0.02),
        'nsa_cmp_v_w1': nrm((DEPTH, cl, HEAD_DIM), cl ** -0.5),
        'nsa_cmp_v_w2': nrm((DEPTH, HEAD_DIM, HEAD_DIM), HEAD_DIM ** -0.5),
        'w_branch': nrm((DEPTH, N_BRANCH, W_MIX, D_MODEL), W_MIX ** -0.5 * BETA_DN),
        'w_out': nrm((DEPTH, D_MODEL, D_MODEL), D_MODEL ** -0.5 * BETA_DN),
        'ln1_g': 1.0 + nrm((DEPTH, D_MODEL), 0.02),
        'ln1_b': nrm((DEPTH, D_MODEL), 0.02),
        'w_up': nrm((DEPTH, D_MODEL, D_FF), D_MODEL ** -0.5),
        'w_down': nrm((DEPTH, D_FF, D_MODEL), D_FF ** -0.5 * BETA_DN),
        'ln2_g': 1.0 + nrm((DEPTH, D_MODEL), 0.02),
        'ln2_b': nrm((DEPTH, D_MODEL), 0.02),
    }


def reference(x_prompt, x_sample, state_gdn_s, state_gdn_conv, cache_fox_k, cache_fox_v, cache_fox_logf,
              cache_nsa_ck, cache_nsa_cv, cache_nsa_sk, cache_nsa_sv, state_nsa_wk, state_nsa_wv,
              cache_moba_k, cache_moba_v, page_table,
              w_in, gdn_conv_w, gdn_a_log, gdn_dt_bias, gdn_norm_g, fox_f_bias,
              nsa_cmp_pos_k, nsa_cmp_k_w1, nsa_cmp_k_w2, nsa_cmp_pos_v, nsa_cmp_v_w1, nsa_cmp_v_w2,
              w_branch, w_out, ln1_g, ln1_b, w_up, w_down, ln2_g, ln2_b):
    sl_nsa, sl_moba = alibi_slopes()
    acc_p = {n: [] for n in STATE_NAMES}
    acc_s = {n: [] for n in STATE_NAMES}
    xp, xs = x_prompt, x_sample
    for l in range(DEPTH):
        lw = dict(w_in=w_in[l], gdn_conv_w=gdn_conv_w[l], gdn_a_log=gdn_a_log[l], gdn_dt_bias=gdn_dt_bias[l],
                  gdn_norm_g=gdn_norm_g[l], fox_f_bias=fox_f_bias[l],
                  nsa_cmp_pos_k=nsa_cmp_pos_k[l], nsa_cmp_k_w1=nsa_cmp_k_w1[l], nsa_cmp_k_w2=nsa_cmp_k_w2[l],
                  nsa_cmp_pos_v=nsa_cmp_pos_v[l], nsa_cmp_v_w1=nsa_cmp_v_w1[l], nsa_cmp_v_w2=nsa_cmp_v_w2[l],
                  w_branch=w_branch[l], w_out=w_out[l], ln1_g=ln1_g[l], ln1_b=ln1_b[l],
                  w_up=w_up[l], w_down=w_down[l], ln2_g=ln2_g[l], ln2_b=ln2_b[l])
        hp = split_proj(xp @ lw['w_in'])
        br_p, new_p = prompt_mixers(hp, lw, sl_nsa, sl_moba)
        xp = finish_layer(xp, br_p, hp['merge_g'], lw)
        past = {
            'gdn_s': state_gdn_s[:, l], 'gdn_conv': state_gdn_conv[:, l],
            'fox_k': gather_pages(cache_fox_k, page_table, l), 'fox_v': gather_pages(cache_fox_v, page_table, l),
            'fox_logf': gather_pages(cache_fox_logf, page_table, l),
            'nsa_ck': gather_pages(cache_nsa_ck, page_table, l), 'nsa_cv': gather_pages(cache_nsa_cv, page_table, l),
            'nsa_sk': gather_pages(cache_nsa_sk, page_table, l), 'nsa_sv': gather_pages(cache_nsa_sv, page_table, l),
            'nsa_wk': state_nsa_wk[:, l], 'nsa_wv': state_nsa_wv[:, l],
            'moba_k': gather_pages(cache_moba_k, page_table, l), 'moba_v': gather_pages(cache_moba_v, page_table, l),
        }
        hs_ = split_proj(xs @ lw['w_in'])
        br_s, new_s = sample_mixers(hs_, past, lw, sl_nsa, sl_moba)
        xs = finish_layer(xs, br_s, hs_['merge_g'], lw)
        for n in STATE_NAMES:
            acc_p[n].append(new_p[n])
            acc_s[n].append(new_s[n])
    sp = {n: jnp.stack(acc_p[n], 1) for n in STATE_NAMES}
    ss = {n: jnp.stack(acc_s[n], 1) for n in STATE_NAMES}
    return (xp, xs,
            sp['gdn_s'], ss['gdn_s'], sp['gdn_conv'], ss['gdn_conv'],
            sp['fox_k'], ss['fox_k'], sp['fox_v'], ss['fox_v'], sp['fox_logf'], ss['fox_logf'],
            sp['nsa_ck'], ss['nsa_ck'], sp['nsa_cv'], ss['nsa_cv'], sp['nsa_sk'], ss['nsa_sk'],
            sp['nsa_sv'], ss['nsa_sv'], sp['nsa_wk'], ss['nsa_wk'], sp['nsa_wv'], ss['nsa_wv'],
            sp['moba_k'], ss['moba_k'], sp['moba_v'], ss['moba_v'])
```

```python
import functools
import math

import jax
import jax.numpy as jnp
from jax import lax
from jax.experimental import pallas as pl
from jax.experimental.pallas import tpu as pltpu

F32 = jnp.float32
BF16 = jnp.bfloat16
NEG = -1e30

HEAD = 64
NH = 4
WM = NH * HEAD
LANE = 128
SUB = 8
PAGE = 128
PAGES_PER_STEP = 8
GDN_CHUNK = 64
CMP_LEN, CMP_STRIDE = 32, 16
SEL_BLOCK, SEL_TOP, NSA_WIN = 64, 16, 512
NSA_FORCE = 1e4
MOBA_BLOCK, MOBA_TOP = 256, 3
LN_EPS = 1e-5
VMEM_LIMIT = 52 * 1024 * 1024

C_MG, C_GQ, C_GK, C_GV, C_GZ = 0, 4096, 4352, 4608, 4864
C_FQ, C_FK, C_FV = 5120, 5376, 5632
C_MQ, C_MK, C_MV = 5888, 6144, 6400
C_NQ, C_NC, C_NS, C_NW, C_SM = 6656, 6912, 7040, 7168, 7296
N_PROJ = 7424
S_GB, S_GA, S_FF, S_NG = 0, 4, 8, 12

SLOPES_NSA = tuple(2.0 ** (-float(i)) for i in (1, 3, 5, 7))
SLOPES_MOBA = tuple(2.0 ** (-float(i)) for i in (2, 4, 6, 8))


def _iota(shape, dim):
    return lax.broadcasted_iota(jnp.int32, shape, dim)


def _dg(a, b, nt=False, tn=False):
    if nt:
        dims = (((1,), (1,)), ((), ()))
    elif tn:
        dims = (((0,), (0,)), ((), ()))
    else:
        dims = (((1,), (0,)), ((), ()))
    return lax.dot_general(a, b, dims, preferred_element_type=F32)


def _split2(a):
    hi = a.astype(BF16)
    lo = (a - hi.astype(F32)).astype(BF16)
    return hi, lo


def _split3(a):
    a1 = a.astype(BF16)
    r = a - a1.astype(F32)
    a2 = r.astype(BF16)
    a3 = (r - a2.astype(F32)).astype(BF16)
    return a1, a2, a3


def mm(a, b, nt=False, tn=False, passes=3):
    if passes == 1:
        return _dg(a.astype(BF16), b.astype(BF16), nt, tn)
    ah, al = _split2(a)
    bh, bl = _split2(b)
    return _dg(ah, bh, nt, tn) + (_dg(ah, bl, nt, tn) + _dg(al, bh, nt, tn))


def mmx(a, b, nt=False, exact="b"):
    if exact == "b":
        bb = b.astype(BF16)
        a1, a2, a3 = _split3(a)
        return (_dg(a1, bb, nt) + _dg(a2, bb, nt)) + _dg(a3, bb, nt)
    aa = a.astype(BF16)
    b1, b2, b3 = _split3(b)
    return (_dg(aa, b1, nt) + _dg(aa, b2, nt)) + _dg(aa, b3, nt)


def _sigmoid(x):
    return 1.0 / (1.0 + jnp.exp(-x))


def _softplus(x):
    return jnp.maximum(x, 0.0) + jnp.log(1.0 + jnp.exp(-jnp.abs(x)))


def _layer_norm(x, g, b):
    mu = jnp.mean(x, -1, keepdims=True)
    xc = x - mu
    var = jnp.mean(xc * xc, -1, keepdims=True)
    return xc * lax.rsqrt(var + LN_EPS) * g + b


def _cparams(sem):
    return pltpu.CompilerParams(dimension_semantics=sem, vmem_limit_bytes=VMEM_LIMIT)


def _full_spec(shape):
    nd = len(shape)
    return pl.BlockSpec(shape, lambda *_: (0,) * nd)


def _proj_kernel(x_ref, w_ref, o_ref, *, passes):
    o_ref[...] = mm(x_ref[...], w_ref[...], passes=passes)


def project(x, w, passes):
    M, K = x.shape
    N = w.shape[1]
    tm = min(M, 1024)
    tn = 256
    return pl.pallas_call(
        functools.partial(_proj_kernel, passes=passes),
        grid=(M // tm, N // tn),
        in_specs=[pl.BlockSpec((tm, K), lambda i, j: (i, 0)), pl.BlockSpec((K, tn), lambda i, j: (0, j))],
        out_specs=pl.BlockSpec((tm, tn), lambda i, j: (i, j)),
        out_shape=jax.ShapeDtypeStruct((M, N), F32),
        compiler_params=_cparams(("parallel", "arbitrary")),
        name="proj",
    )(x, w)


def _mix_kernel(x_ref, ba_ref, bb_ref, bc_ref, bd_ref, g_ref, wb_ref, wo_ref, lg_ref, lb_ref, o_ref, *, alpha):
    D = x_ref.shape[1]
    acc = None
    for n, br in enumerate((ba_ref, bb_ref, bc_ref, bd_ref)):
        up = _dg(br[...].astype(BF16), wb_ref[n])
        t = _sigmoid(g_ref[:, n * D:(n + 1) * D]) * up
        acc = t if acc is None else acc + t
    mixed = _dg(acc.astype(BF16), wo_ref[...])
    o_ref[...] = _layer_norm(alpha * x_ref[...] + mixed, lg_ref[...], lb_ref[...])


def mix_layer(x, branches, h, w_branch, w_out, ln_g, ln_b, alpha):
    M, D = x.shape
    tm = min(M, 256)
    row = lambda i: (i, 0)
    return pl.pallas_call(
        functools.partial(_mix_kernel, alpha=alpha),
        grid=(M // tm,),
        in_specs=[pl.BlockSpec((tm, D), row)] + [pl.BlockSpec((tm, WM), row)] * 4
        + [pl.BlockSpec((tm, NH * D), lambda i: (i, C_MG // (NH * D))),
           _full_spec(w_branch.shape), _full_spec(w_out.shape), _full_spec((1, D)), _full_spec((1, D))],
        out_specs=pl.BlockSpec((tm, D), row),
        out_shape=jax.ShapeDtypeStruct((M, D), F32),
        compiler_params=_cparams(("parallel",)),
        name="mix",
    )(x, *branches, h, w_branch, w_out, ln_g.reshape(1, D), ln_b.reshape(1, D))


def _mlp_kernel(x_ref, wu_ref, wd_ref, lg_ref, lb_ref, o_ref, acc_ref, *, alpha):
    f = pl.program_id(1)

    @pl.when(f == 0)
    def _():
        acc_ref[...] = jnp.zeros_like(acc_ref)

    hmid = jnp.square(jnp.maximum(_dg(x_ref[...].astype(BF16), wu_ref[...]), 0.0))
    acc_ref[...] += _dg(hmid.astype(BF16), wd_ref[...])

    @pl.when(f == pl.num_programs(1) - 1)
    def _():
        o_ref[...] = _layer_norm(alpha * x_ref[...] + acc_ref[...], lg_ref[...], lb_ref[...])


def mlp_layer(x, w_up, w_down, ln_g, ln_b, alpha):
    M, D = x.shape
    FF = w_up.shape[1]
    tm = min(M, 1024)
    tf = 512
    return pl.pallas_call(
        functools.partial(_mlp_kernel, alpha=alpha),
        grid=(M // tm, FF // tf),
        in_specs=[pl.BlockSpec((tm, D), lambda i, f: (i, 0)), pl.BlockSpec((D, tf), lambda i, f: (0, f)),
                  pl.BlockSpec((tf, D), lambda i, f: (f, 0)), _full_spec((1, D)), _full_spec((1, D))],
        out_specs=pl.BlockSpec((tm, D), lambda i, f: (i, 0)),
        out_shape=jax.ShapeDtypeStruct((M, D), F32),
        scratch_shapes=[pltpu.VMEM((tm, D), F32)],
        compiler_params=_cparams(("parallel", "arbitrary")),
        name="mlp",
    )(x, w_up, w_down, ln_g.reshape(1, D), ln_b.reshape(1, D))


def _cum_kernel(x_ref, bias_ref, c0_ref, lf_ref, c_ref, carry_ref, *, apply_ls):
    tb = x_ref.shape[2]

    @pl.when(pl.program_id(1) == 0)
    def _():
        carry_ref[...] = c0_ref[0]

    x = x_ref[0]
    if apply_ls:
        z = x + bias_ref[:, 0:1]
        x = jnp.minimum(z, 0.0) - jnp.log(1.0 + jnp.exp(-jnp.abs(z)))
    lf_ref[0] = x
    upper = (_iota((tb, tb), 0) <= _iota((tb, tb), 1)).astype(F32)
    cs = mmx(x, upper, exact="b") + carry_ref[:, 0:1]
    c_ref[0] = cs
    carry_ref[...] = jnp.broadcast_to(cs[:, tb - 1:tb], carry_ref.shape)


def cum_logf(x_row, bias, c0, apply_ls):
    B, R, T = x_row.shape
    tb = min(T, 512)
    return pl.pallas_call(
        functools.partial(_cum_kernel, apply_ls=apply_ls),
        grid=(B, T // tb),
        in_specs=[pl.BlockSpec((1, R, tb), lambda b, t: (b, 0, t)), _full_spec((R, LANE)),
                  pl.BlockSpec((1, R, LANE), lambda b, t: (b, 0, 0))],
        out_specs=[pl.BlockSpec((1, R, tb), lambda b, t: (b, 0, t))] * 2,
        out_shape=[jax.ShapeDtypeStruct((B, R, T), F32)] * 2,
        scratch_shapes=[pltpu.VMEM((R, LANE), F32)],
        compiler_params=_cparams(("parallel", "arbitrary")),
        name="cum_logf",
    )(x_row, bias, c0)


def _q3(q):
    qh, ql = _split2(q)
    return jnp.concatenate([qh, ql, qh], axis=1)


def _k3(k):
    kh, kl = _split2(k)
    return jnp.concatenate([kh, kh, kl], axis=1)


def _attend(q3, sources, nrows, dv):
    carry = (jnp.full((nrows, 1), NEG, F32), jnp.zeros((nrows, 1), F32), jnp.zeros((nrows, dv), F32))
    for k3_tile, v_tile, bias, mask, j_lo, j_hi in sources:
        def body(j, carry, k3_tile=k3_tile, v_tile=v_tile, bias=bias, mask=mask):
            m, l, acc = carry
            s = _dg(q3, k3_tile(j), nt=True) + bias(j)
            if mask is not None:
                keep = mask(j)
                s = jnp.where(keep, s, NEG)
            m_new = jnp.maximum(m, jnp.max(s, -1, keepdims=True))
            alpha = jnp.exp(m - m_new)
            e = jnp.exp(s - m_new)
            if mask is not None:
                e = jnp.where(keep, e, 0.0)
            l = alpha * l + jnp.sum(e, -1, keepdims=True)
            acc = alpha * acc + _dg(e.astype(BF16), v_tile(j))
            return m_new, l, acc
        if isinstance(j_lo, int) and isinstance(j_hi, int) and j_hi - j_lo == 1:
            carry = body(j_lo, carry)
        else:
            carry = lax.fori_loop(j_lo, j_hi, body, carry)
    m, l, acc = carry
    return acc / jnp.maximum(l, 1e-30)


def _stack_heads_q(q, tq):
    keep = (_iota((NH * tq, 1), 0) // tq) == (_iota((1, WM), 1) // HEAD)
    return jnp.where(keep, jnp.concatenate([q] * NH, axis=0), 0.0)


def _unstack_heads(o, tq):
    lane_head = _iota((1, WM), 1) // HEAD
    out = o[0:tq]
    for h in range(1, NH):
        out = jnp.where(lane_head == h, o[h * tq:(h + 1) * tq], out)
    return out


def _per_head_rows(fn, tq):
    return jnp.concatenate([fn(h, slice(h * tq, (h + 1) * tq)) for h in range(NH)], axis=0)


def _rank_lt(score, n, top):
    lane = _iota(score.shape, 1)
    rank = jnp.zeros(score.shape, F32)
    for s2 in range(n):
        col = score[:, s2:s2 + 1]
        ahead = (col > score) | ((col == score) & (s2 < lane))
        rank = rank + ahead.astype(F32)
    return rank < top


def _kv_tiles(k_ref, v_ref, tk, scratch, first_step):
    n_main = k_ref.shape[0] // tk
    tile = lambda ref, j: ref[pl.ds(pl.multiple_of(j * tk, tk), tk), :]
    if scratch is None:
        return (lambda j: _k3(tile(k_ref, j))), (lambda j: tile(v_ref, j).astype(BF16))
    k3_ref, vb_ref = scratch

    @pl.when(first_step)
    def _():
        for j in range(n_main):
            rows = slice(j * tk, (j + 1) * tk)
            k3_ref[rows, :] = _k3(k_ref[rows, :])
            vb_ref[rows, :] = v_ref[rows, :].astype(BF16)

    return (lambda j: tile(k3_ref, j)), (lambda j: tile(vb_ref, j))


def _fox_kernel(*refs, tq, tk, qoff, new_off, cache_kv):
    has_new = new_off is not None
    refs = list(refs)
    q_ref, cq_ref, k_ref, v_ref, ck_ref = refs[:5]
    kn_ref, vn_ref, ckn_ref = refs[5:8] if has_new else (None, None, None)
    o_ref = refs[8 if has_new else 5]
    scratch = tuple(refs[-2:]) if cache_kv else None
    qb = pl.program_id(1)
    R = NH * tq
    n_main = k_ref.shape[0] // tk
    k3_tile, v_tile = _kv_tiles(k_ref, v_ref, tk, scratch, qb == 0)
    q0 = qoff + qb * tq
    qpos = q0 + _iota((R, 1), 0) % tq
    q3 = _q3(_stack_heads_q(q_ref[...] * (HEAD ** -0.5), tq))
    cq = jnp.concatenate([cq_ref[0, :, h:h + 1] for h in range(NH)], axis=0)

    def bias_main(j):
        ck = ck_ref[0, j]
        return _per_head_rows(lambda h, rows: cq[rows] - ck[h:h + 1, :], tq)

    mask_main = lambda j: (j * tk + _iota((1, tk), 1)) <= qpos
    j_full = jnp.minimum(n_main, q0 // tk)
    j_hi = jnp.minimum(n_main, (q0 + tq - 1) // tk + 1)
    srcs = [(k3_tile, v_tile, bias_main, None, 0, j_full), (k3_tile, v_tile, bias_main, mask_main, j_full, j_hi)]
    if has_new:
        bias_new = lambda j: _per_head_rows(lambda h, rows: cq[rows] - ckn_ref[0, h:h + 1, :], tq)
        mask_new = lambda j: (new_off + _iota((1, SUB), 1)) <= qpos
        srcs.append((lambda j: _k3(kn_ref[...]), lambda j: vn_ref[...].astype(BF16), bias_new, mask_new, 0, 1))
    o_ref[...] = _unstack_heads(_attend(q3, srcs, R, WM), tq)


def fox_attention(hq, hk, nb, T, cq_col, ck_main, new=None, qoff=0):
    tq = min(T, 128)
    nq = T // tq
    qspec = pl.BlockSpec((tq, WM), lambda b, i: (b * nq + i, C_FQ // WM))
    cqspec = pl.BlockSpec((1, tq, NH), lambda b, i: (b, i, 0))
    tk = ck_main.shape[-1]
    if new is None:
        L = T
        ins = [hq, cq_col, hk, hk, ck_main]
        specs = [qspec, cqspec, pl.BlockSpec((L, WM), lambda b, i: (b, C_FK // WM)),
                 pl.BlockSpec((L, WM), lambda b, i: (b, C_FV // WM)),
                 pl.BlockSpec((1, L // tk, SUB, tk), lambda b, i: (b, 0, 0, 0))]
        new_off = None
    else:
        k_main, v_main, ck_new, new_off = new
        L = k_main.shape[1]
        ins = [hq, cq_col, k_main.reshape(nb * L, WM), v_main.reshape(nb * L, WM), ck_main, hq, hq, ck_new]
        specs = [qspec, cqspec, pl.BlockSpec((L, WM), lambda b, i: (b, 0)), pl.BlockSpec((L, WM), lambda b, i: (b, 0)),
                 pl.BlockSpec((1, L // tk, SUB, tk), lambda b, i: (b, 0, 0, 0)),
                 pl.BlockSpec((tq, WM), lambda b, i: (b * nq + i, C_FK // WM)),
                 pl.BlockSpec((tq, WM), lambda b, i: (b * nq + i, C_FV // WM)),
                 pl.BlockSpec((1, SUB, SUB), lambda b, i: (b, 0, 0))]
        assert tq == SUB and nq == 1
    cache_kv = nq > 1
    scratch = [pltpu.VMEM((L, 3 * WM), BF16), pltpu.VMEM((L, WM), BF16)] if cache_kv else []
    return pl.pallas_call(
        functools.partial(_fox_kernel, tq=tq, tk=tk, qoff=qoff, new_off=new_off, cache_kv=cache_kv),
        grid=(nb, nq),
        in_specs=specs,
        out_specs=pl.BlockSpec((tq, WM), lambda b, i: (b * nq + i, 0)),
        out_shape=jax.ShapeDtypeStruct((nb * T, WM), F32),
        scratch_shapes=scratch,
        compiler_params=_cparams(("parallel", "arbitrary")),
        name="fox",
    )(*ins)


def _moba_kernel(*refs, tq, tk, qoff, new_off, cache_kv):
    has_new = new_off is not None
    refs = list(refs)
    q_ref, k_ref, v_ref = refs[:3]
    kn_ref, vn_ref = refs[3:5] if has_new else (None, None)
    o_ref, kmean_ref = refs[5:7] if has_new else refs[3:5]
    scratch = tuple(refs[-2:]) if cache_kv else None
    qb = pl.program_id(1)
    R = NH * tq
    L = k_ref.shape[0]
    nblk = L // MOBA_BLOCK
    n_main = L // tk
    blocks_per_tile = tk // MOBA_BLOCK

    @pl.when(qb == 0)
    def _():
        kmean_ref[...] = jnp.zeros_like(kmean_ref)
        for n in range(nblk):
            blk = k_ref[n * MOBA_BLOCK:(n + 1) * MOBA_BLOCK, :]
            kmean_ref[n:n + 1, :] = jnp.sum(blk, axis=0, keepdims=True) * (1.0 / MOBA_BLOCK)

    k3_tile, v_tile = _kv_tiles(k_ref, v_ref, tk, scratch, qb == 0)
    q0 = qoff + qb * tq
    qpos = q0 + _iota((R, 1), 0) % tq
    cur = qpos // MOBA_BLOCK
    qbd = _stack_heads_q(q_ref[...], tq)
    q3 = _q3(qbd * (HEAD ** -0.5))
    row_head = _iota((R, 1), 0) // tq
    slope = jnp.zeros((R, 1), F32)
    for h in range(NH):
        slope = jnp.where(row_head == h, SLOPES_MOBA[h], slope)
    col_q = -slope * qpos.astype(F32)

    blk_lane = _iota((1, LANE), 1)
    gs = mm(qbd, kmean_ref[...], nt=True)
    gs = jnp.where((blk_lane < cur) & (blk_lane < nblk), gs, NEG)
    sel = _rank_lt(gs, nblk, MOBA_TOP) & (gs > 0.5 * NEG)
    sel_bias = jnp.where(sel, 0.0, NEG)

    def alibi(kpos, col):
        kposf = kpos.astype(F32)
        return _per_head_rows(lambda h, rows: SLOPES_MOBA[h] * kposf + col[rows], tq)

    def bias_main(j):
        parts = []
        for i in range(blocks_per_tile):
            b = j * blocks_per_tile + i
            col_sel = jnp.sum(jnp.where(blk_lane == b, sel_bias, 0.0), axis=1, keepdims=True)
            parts.append(alibi(b * MOBA_BLOCK + _iota((1, MOBA_BLOCK), 1), col_sel + col_q))
        return parts[0] if blocks_per_tile == 1 else jnp.concatenate(parts, axis=1)

    srcs = [(k3_tile, v_tile, bias_main, None, 0, jnp.minimum(n_main, q0 // tk))]
    if has_new:
        kpos_own = new_off + _iota((1, SUB), 1)
        own = (lambda j: _k3(kn_ref[...]), lambda j: vn_ref[...].astype(BF16))
    else:
        assert blocks_per_tile == 1 and MOBA_BLOCK % tq == 0
        j_own = q0 // tk
        kpos_own = j_own * tk + _iota((1, tk), 1)
        own = (lambda j: k3_tile(j_own), lambda j: v_tile(j_own))
    mask_own = lambda j: ((kpos_own // MOBA_BLOCK) == cur) & (kpos_own <= qpos)
    srcs.append((own[0], own[1], lambda j: alibi(kpos_own, col_q), mask_own, 0, 1))
    o_ref[...] = _unstack_heads(_attend(q3, srcs, R, WM), tq)


def moba_attention(hq, nb, T, new=None, qoff=0):
    tq = min(T, 128)
    nq = T // tq
    qspec = pl.BlockSpec((tq, WM), lambda b, i: (b * nq + i, C_MQ // WM))
    if new is None:
        L = T
        ins = [hq, hq, hq]
        specs = [qspec, pl.BlockSpec((L, WM), lambda b, i: (b, C_MK // WM)),
                 pl.BlockSpec((L, WM), lambda b, i: (b, C_MV // WM))]
        new_off = None
    else:
        k_main, v_main, new_off = new
        L = k_main.shape[1]
        ins = [hq, k_main.reshape(nb * L, WM), v_main.reshape(nb * L, WM), hq, hq]
        specs = [qspec, pl.BlockSpec((L, WM), lambda b, i: (b, 0)), pl.BlockSpec((L, WM), lambda b, i: (b, 0)),
                 pl.BlockSpec((tq, WM), lambda b, i: (b * nq + i, C_MK // WM)),
                 pl.BlockSpec((tq, WM), lambda b, i: (b * nq + i, C_MV // WM))]
        assert tq == SUB and nq == 1
    tk = MOBA_BLOCK if new is None else min(L, 4 * MOBA_BLOCK)
    assert L // MOBA_BLOCK <= LANE
    cache_kv = nq > 1
    scratch = [pltpu.VMEM((L, 3 * WM), BF16), pltpu.VMEM((L, WM), BF16)] if cache_kv else []
    return pl.pallas_call(
        functools.partial(_moba_kernel, tq=tq, tk=tk, qoff=qoff, new_off=new_off, cache_kv=cache_kv),
        grid=(nb, nq),
        in_specs=specs,
        out_specs=pl.BlockSpec((tq, WM), lambda b, i: (b * nq + i, 0)),
        out_shape=jax.ShapeDtypeStruct((nb * T, WM), F32),
        scratch_shapes=[pltpu.VMEM((LANE, WM), F32)] + scratch,
        compiler_params=_cparams(("parallel", "arbitrary")),
        name="moba",
    )(*ins)


def _compress_kernel(g_ref, pos_ref, w1_ref, w2_ref, o_ref):
    half = g_ref.shape[1]
    ng = g_ref.shape[0]
    g = g_ref[...]
    a = mm(g + pos_ref[0:1, :], w1_ref[0:half, :])
    b = mm(g + pos_ref[1:2, :], w1_ref[half:2 * half, :])
    shift = (_iota((ng, ng), 1) == _iota((ng, ng), 0) + 1).astype(F32)
    pre = a + mmx(shift, b, exact="a")
    act = pre * _sigmoid(pre)
    o_ref[...] = mm(act, w2_ref[...])


def nsa_compress(rows_pair, nb, pos_pair, w1_pair, w2_pair):
    L = rows_pair.shape[0] // nb
    ng = L // CMP_STRIDE
    half = CMP_STRIDE * LANE
    g = rows_pair.reshape(nb * ng, half)
    return pl.pallas_call(
        _compress_kernel,
        grid=(nb,),
        in_specs=[pl.BlockSpec((ng, half), lambda b: (b, 0)), _full_spec(pos_pair.shape), _full_spec(w1_pair.shape),
                  _full_spec(w2_pair.shape)],
        out_specs=pl.BlockSpec((ng, LANE), lambda b: (b, 0)),
        out_shape=jax.ShapeDtypeStruct((nb * ng, LANE), F32),
        compiler_params=_cparams(("parallel",)),
        name="nsa_compress",
    )(g, pos_pair, w1_pair, w2_pair)


def _nsa_kernel(*refs, tq, tk, qoff, n_sel, sel_srcs, win_srcs, cache_kv):
    n_s, n_w = len(sel_srcs), len(win_srcs)
    q_ref, sm_ref, kc_ref = refs[:3]
    sel_refs = refs[3:3 + n_s]
    win_refs = refs[3 + n_s:3 + n_s + n_w]
    o_ref = refs[3 + n_s + n_w]
    scratch = refs[4 + n_s + n_w:] if cache_kv else (None, None)
    passes = 3
    qb = pl.program_id(1)
    R = NH * tq
    q0 = qoff + qb * tq
    qpos1 = q0 + _iota((tq, 1), 0)
    qpos = jnp.concatenate([qpos1] * NH, axis=0)
    row_head = _iota((R, 1), 0) // tq
    slope = jnp.zeros((R, 1), F32)
    for h in range(NH):
        slope = jnp.where(row_head == h, SLOPES_NSA[h], slope)
    lane = _iota((1, LANE), 1)
    scale = HEAD ** -0.5

    qs = []
    for h in range(NH):
        pair = q_ref[:, (h // 2) * LANE:(h // 2 + 1) * LANE]
        if h % 2:
            pair = pltpu.roll(pair, HEAD, axis=1)
        qs.append(jnp.where(lane < HEAD, pair, 0.0))
    q = jnp.concatenate(qs, axis=0)

    nc = kc_ref.shape[0]
    kcv = kc_ref[...]
    cend = _iota((1, nc), 1) * CMP_STRIDE + (CMP_LEN - 1)
    s_c = mm(q, kcv, nt=True, passes=passes) * scale - slope * (qpos - cend).astype(F32)
    m_c = cend <= qpos
    z = jnp.where(m_c, s_c, NEG)
    e = jnp.where(m_c, jnp.exp(z - jnp.max(z, -1, keepdims=True)), 0.0)
    p_c = e / jnp.maximum(jnp.sum(e, -1, keepdims=True), 1e-30)
    o_c = mm(p_c, kcv, passes=passes)

    n_sel_p = -(-n_sel // LANE) * LANE
    p_sum = p_c[0:tq]
    for h in range(1, NH):
        p_sum = p_sum + p_c[h * tq:(h + 1) * tq]
    cn = _iota((nc, n_sel_p), 0) * CMP_STRIDE
    sb = _iota((nc, n_sel_p), 1)
    overlap = ((cn < (sb + 1) * SEL_BLOCK) & (cn + (CMP_LEN - 1) >= sb * SEL_BLOCK)).astype(F32)
    imp = mmx(p_sum, overlap, exact="b")
    blk = _iota((1, n_sel_p), 1)
    cur = qpos1 // SEL_BLOCK
    valid = blk <= cur
    forced = (blk == 0) | (blk == cur) | (blk == cur - 1)
    score = jnp.where(valid, jnp.where(forced, NSA_FORCE, imp), NEG)
    sel1 = (_rank_lt(score, n_sel, SEL_TOP) & (score > 0.5 * NEG)).astype(F32)

    q3 = _q3(q * scale)
    col_q = -slope * qpos.astype(F32)

    def make_srcs(ref, k3_ref, pos0, nrows, kind):
        t = min(tk, nrows)
        n_t = nrows // t
        if k3_ref is None:
            raw = (lambda j: ref[...]) if n_t == 1 else (lambda j: ref[pl.ds(pl.multiple_of(j * t, t), t), :])
            k3_tile = lambda j: _k3(raw(j))
            v_tile = lambda j: raw(j).astype(BF16)
        else:
            @pl.when(qb == 0)
            def _():
                for j in range(n_t):
                    rows = slice(j * t, (j + 1) * t)
                    k3_ref[rows, :] = _k3(ref[rows, :])

            k3_tile = lambda j: k3_ref[pl.ds(pl.multiple_of(j * t, t), t), :]
            v_tile = lambda j: k3_ref[pl.ds(pl.multiple_of(j * t, t), t), 0:LANE]

        def kpos(j):
            return pos0 + j * t + _iota((1, t), 1)

        def alibi(j):
            kposf = kpos(j).astype(F32)
            return _per_head_rows(lambda h, rows: SLOPES_NSA[h] * kposf + col_q[rows], tq)

        def chosen(j):
            kb = (pos0 + j * t + _iota((n_sel_p, t), 1)) // SEL_BLOCK
            expand = (_iota((n_sel_p, t), 0) == kb).astype(BF16)
            return _dg(sel1.astype(BF16), expand)

        if kind == "win":
            def mask(j):
                dist = qpos - kpos(j)
                return (dist >= 0) & (dist < NSA_WIN) & (kpos(j) >= 0)
            if n_t == 1:
                return [(k3_tile, v_tile, alibi, mask, 0, 1)]
            lo = jnp.maximum(0, (q0 - (NSA_WIN - 1) - pos0) // t)
            hi = jnp.minimum(n_t, (q0 + tq - 1 - pos0) // t + 1)
            return [(k3_tile, v_tile, alibi, mask, lo, hi)]

        def mask(j):
            keep = jnp.concatenate([chosen(j) > 0.5] * NH, axis=0)
            return keep & (kpos(j) <= qpos)

        if n_t == 1:
            return [(k3_tile, v_tile, alibi, mask, 0, 1)]

        def bias_past(j):
            excl = (chosen(j) - 1.0) * (-NEG)
            kposf = kpos(j).astype(F32)
            return _per_head_rows(lambda h, rows: (excl + SLOPES_NSA[h] * kposf) + col_q[rows], tq)

        j_full = jnp.minimum(n_t, jnp.maximum(0, q0 - pos0) // t)
        j_hi = jnp.minimum(n_t, (q0 + tq - 1 - pos0) // t + 1)
        return [(k3_tile, v_tile, bias_past, None, 0, j_full), (k3_tile, v_tile, alibi, mask, j_full, j_hi)]

    sel_sources, win_sources = [], []
    for i, (r, (p0, n)) in enumerate(zip(sel_refs, sel_srcs)):
        sel_sources += make_srcs(r, scratch[0] if i == 0 else None, p0, n, "sel")
    for i, (r, (p0, n)) in enumerate(zip(win_refs, win_srcs)):
        win_sources += make_srcs(r, scratch[1] if i == 0 else None, p0, n, "win")
    o_g = _attend(q3, sel_sources, R, LANE)
    o_w = _attend(q3, win_sources, R, LANE)

    gates = _sigmoid(sm_ref[...])

    def gate_col(i):
        return jnp.concatenate([gates[:, S_NG + 3 * h + i:S_NG + 3 * h + i + 1] for h in range(NH)], axis=0)

    out = gate_col(0) * o_c + gate_col(1) * o_g + gate_col(2) * o_w
    for p in range(NH // 2):
        even = pltpu.roll(out[(2 * p) * tq:(2 * p + 1) * tq], HEAD, axis=1)
        odd = out[(2 * p + 1) * tq:(2 * p + 2) * tq]
        o_ref[:, p * LANE:(p + 1) * LANE] = jnp.where(lane < HEAD, even, odd)


def nsa_attention(hq, nb, T, kcvc, sel_list, win_list, n_sel, qoff=0):
    tq = min(T, 128)
    nq = T // tq
    nc = kcvc.shape[0] // nb
    ins = [hq, hq, kcvc]
    specs = [pl.BlockSpec((tq, WM), lambda b, i: (b * nq + i, C_NQ // WM)),
             pl.BlockSpec((tq, LANE), lambda b, i: (b * nq + i, C_SM // LANE)),
             pl.BlockSpec((nc, LANE), lambda b, i: (b, 0))]
    for arr, col, _, rows in list(sel_list) + list(win_list):
        ins.append(arr)
        specs.append(pl.BlockSpec((rows, LANE), lambda b, i, col=col: (b, col)))
    cache_kv = nq > 1
    kern = functools.partial(
        _nsa_kernel, tq=tq, tk=512 if cache_kv else 1024, qoff=qoff, n_sel=n_sel,
        sel_srcs=tuple((p0, rows) for _, _, p0, rows in sel_list),
        win_srcs=tuple((p0, rows) for _, _, p0, rows in win_list), cache_kv=cache_kv)
    scratch = [pltpu.VMEM((sel_list[0][3], 3 * LANE), BF16), pltpu.VMEM((win_list[0][3], 3 * LANE), BF16)] if cache_kv else []
    return pl.pallas_call(
        kern,
        grid=(nb, nq),
        in_specs=specs,
        out_specs=pl.BlockSpec((tq, WM), lambda b, i: (b * nq + i, 0)),
        out_shape=jax.ShapeDtypeStruct((nb * T, WM), F32),
        scratch_shapes=scratch,
        compiler_params=_cparams(("parallel", "arbitrary")),
        name="nsa",
    )(*ins)


def _gdn_kernel(q_ref, k_ref, v_ref, z_ref, sm_ref, cwq_ref, cwk_ref, cwv_ref, cbq_ref, cbk_ref, cbv_ref,
                s0_ref, par_ref, ng_ref, o_ref, sout_ref, xq_ref, xk_ref, xv_ref, s_ref, *, C, t_valid, passes):
    c = pl.program_id(1)
    R = NH * C
    bd = (_iota((WM, WM), 0) // HEAD) == (_iota((WM, WM), 1) // HEAD)
    bd_f = bd.astype(F32)

    @pl.when(c == 0)
    def _():
        xq_ref[0:SUB, :] = cbq_ref[0]
        xk_ref[0:SUB, :] = cbk_ref[0]
        xv_ref[0:SUB, :] = cbv_ref[0]
        spread = ((_iota((HEAD, WM), 1) % HEAD) == _iota((HEAD, WM), 0)).astype(F32)
        s_ref[...] = jnp.where(bd, mmx(s0_ref[0], spread, exact="b"), 0.0)

    def conv(x_ref, ext_ref, cw_ref):
        ext_ref[SUB:SUB + C, :] = x_ref[...]
        y = cw_ref[3:4, :] * ext_ref[SUB:SUB + C, :]
        for j in range(3):
            y = y + cw_ref[j:j + 1, :] * ext_ref[SUB - 3 + j:SUB - 3 + j + C, :]
        ext_ref[0:SUB, :] = ext_ref[C:C + SUB, :]
        return y * _sigmoid(y)

    qc = conv(q_ref, xq_ref, cwq_ref)
    kc = conv(k_ref, xk_ref, cwk_ref)
    vc = conv(v_ref, xv_ref, cwv_ref)
    qn = qc * lax.rsqrt(mmx(qc * qc, bd_f, exact="b") + 1e-6) * (HEAD ** -0.5)
    kn = kc * lax.rsqrt(mmx(kc * kc, bd_f, exact="b") + 1e-6)

    sm = sm_ref[...]
    tok_ok = (c * C + _iota((C, 1), 0)) < t_valid
    beta_t = jnp.where(tok_ok, _sigmoid(sm), 0.0)
    g_t = jnp.where(tok_ok, -jnp.exp(par_ref[0:1, :]) * _softplus(sm + par_ref[1:2, :]), 0.0)
    lower_c = (_iota((C, C), 1) <= _iota((C, C), 0)).astype(F32)
    gam_t = mmx(lower_c, g_t, exact="a")

    def stack_col(tile, lane0):
        return jnp.concatenate([tile[:, lane0 + h:lane0 + h + 1] for h in range(NH)], axis=0)

    bcol = stack_col(beta_t, S_GB)
    gcol = stack_col(gam_t, S_GA)
    gl_row = gam_t[C - 1:C, :]
    glcol = jnp.concatenate([jnp.broadcast_to(gl_row[:, S_GA + h:S_GA + h + 1], (C, 1)) for h in range(NH)], axis=0)
    gl_state = jnp.concatenate([jnp.broadcast_to(gl_row[:, S_GA + h:S_GA + h + 1], (HEAD, 1)) for h in range(NH)], axis=0)
    pick0 = (_iota((SUB, LANE), 1) == 0).astype(F32)
    grow = mmx(pick0, jnp.broadcast_to(gcol, (R, LANE)), nt=True, exact="a")[0:1, :]

    rr = _iota((R, R), 0)
    cc = _iota((R, R), 1)
    same = (rr // C) == (cc // C)
    lower = same & (cc <= rr)
    dec = jnp.where(lower, jnp.exp(jnp.where(lower, gcol - grow, 0.0)), 0.0)

    lane_head = _iota((1, WM), 1) // HEAD

    def stack_heads(x):
        return jnp.concatenate([jnp.where(lane_head == h, x, 0.0) for h in range(NH)], axis=0)

    kst, qst, vst = stack_heads(kn), stack_heads(qn), stack_heads(vc)
    a_mat = bcol * jnp.where(cc < rr, dec, 0.0) * mm(kst, kst, nt=True, passes=passes)
    t_inv = (rr == cc).astype(F32) - a_mat
    pw = a_mat
    for _ in range(int(math.log2(C)) - 1):
        pw = mm(pw, pw, passes=passes)
        t_inv = t_inv + mm(t_inv, pw, passes=passes)
    egam = jnp.exp(gcol)
    w_m = mm(t_inv, (bcol * egam) * kst, passes=passes)
    u_m = mm(t_inv, bcol * vst, passes=passes)
    p_m = dec * mm(qst, kst, nt=True, passes=passes)
    qg = egam * qst
    kd = jnp.exp(glcol - gcol) * kst

    s = s_ref[...]
    u2 = u_m - mm(w_m, s, passes=passes)
    o_st = mm(qg, s, passes=passes) + mm(p_m, u2, passes=passes)
    s_new = jnp.exp(gl_state) * s + mm(kd, u2, tn=True, passes=passes)
    s_ref[...] = s_new

    o = o_st[0:C]
    for h in range(1, NH):
        o = o + o_st[h * C:(h + 1) * C]
    o = o * lax.rsqrt(mmx(o * o, bd_f, exact="b") * (1.0 / HEAD) + 1e-6)
    zt = z_ref[...]
    o_ref[...] = o * ng_ref[...] * (zt * _sigmoid(zt))

    @pl.when(c == pl.num_programs(1) - 1)
    def _():
        gather = ((_iota((WM, HEAD), 0) % HEAD) == _iota((WM, HEAD), 1)).astype(F32)
        sout_ref[0] = mmx(s_new, gather, exact="b")


def gdn_branch(h, nb, T, t_valid, conv_w, conv_buf, s0, a_log, dt_bias, norm_g, passes=3):
    C = min(T, GDN_CHUNK)
    nchunk = T // C
    row = lambda col: pl.BlockSpec((C, WM), lambda b, c, col=col: (b * nchunk + c, col // WM))
    cb = jnp.pad(conv_buf, ((0, 0), (SUB - 3, 0), (0, 0)))
    par = jnp.zeros((SUB, LANE), F32).at[0, S_GA:S_GA + NH].set(a_log).at[1, S_GA:S_GA + NH].set(dt_bias)
    ng = jnp.tile(norm_g, NH).reshape(1, WM)
    cws = [conv_w[:, i * WM:(i + 1) * WM] for i in range(3)]
    cb_spec = lambda i: pl.BlockSpec((1, SUB, WM), lambda b, c, i=i: (b, 0, i))
    o, s_new = pl.pallas_call(
        functools.partial(_gdn_kernel, C=C, t_valid=t_valid, passes=passes),
        grid=(nb, nchunk),
        in_specs=[row(C_GQ), row(C_GK), row(C_GV), row(C_GZ),
                  pl.BlockSpec((C, LANE), lambda b, c: (b * nchunk + c, C_SM // LANE))]
        + [_full_spec((4, WM))] * 3 + [cb_spec(0), cb_spec(1), cb_spec(2),
                                        pl.BlockSpec((1, WM, HEAD), lambda b, c: (b, 0, 0)),
                                        _full_spec((SUB, LANE)), _full_spec((1, WM))],
        out_specs=[pl.BlockSpec((C, WM), lambda b, c: (b * nchunk + c, 0)),
                   pl.BlockSpec((1, WM, HEAD), lambda b, c: (b, 0, 0))],
        out_shape=[jax.ShapeDtypeStruct((nb * T, WM), F32), jax.ShapeDtypeStruct((nb, WM, HEAD), F32)],
        scratch_shapes=[pltpu.VMEM((SUB + C, WM), F32)] * 3 + [pltpu.VMEM((WM, WM), F32)],
        compiler_params=_cparams(("parallel", "arbitrary")),
        name="gdn",
    )(h, h, h, h, h, *cws, cb, cb, cb, s0.reshape(nb, WM, HEAD), par, ng)
    return o, s_new.reshape(nb, NH, HEAD, HEAD)


def _gather_kernel(pt_ref, *refs, n_src, width):
    del pt_ref
    G = PAGES_PER_STEP
    o_ref = refs[n_src * G]
    for g in range(G):
        rows = slice(g * PAGE, (g + 1) * PAGE)
        if n_src == 1:
            o_ref[0, rows, :] = refs[g][0, 0]
        else:
            out = None
            for s in range(n_src):
                place = (_iota((width, n_src * width), 1) == _iota((width, n_src * width), 0) + s * width).astype(F32)
                t = mmx(refs[s * G + g][0, 0], place, exact="b")
                out = t if out is None else out + t
            o_ref[0, rows, :] = out


def gather_pages(caches, page_table, layer):
    nb, n_pages = page_table.shape
    G = PAGES_PER_STEP
    w = caches[0].shape[-1]
    n_src = len(caches)
    ins, specs = [], []
    for cache in caches:
        for g in range(G):
            ins.append(cache)
            specs.append(pl.BlockSpec((1, 1, PAGE, w), lambda b, i, pt, g=g: (pt[b, i * G + g], layer, 0, 0)))
    return pl.pallas_call(
        functools.partial(_gather_kernel, n_src=n_src, width=w),
        grid_spec=pltpu.PrefetchScalarGridSpec(
            num_scalar_prefetch=1, grid=(nb, n_pages // G), in_specs=specs,
            out_specs=pl.BlockSpec((1, G * PAGE, n_src * w), lambda b, i, pt: (b, i, 0))),
        out_shape=jax.ShapeDtypeStruct((nb, n_pages * PAGE, n_src * w), F32),
        compiler_params=_cparams(("parallel", "arbitrary")),
        name="gather_pages",
    )(page_table, *ins)


def _prep_layer(l, w_in, nsa_cmp_pos_k, nsa_cmp_k_w1, nsa_cmp_k_w2, nsa_cmp_pos_v, nsa_cmp_v_w1, nsa_cmp_v_w2):
    w = w_in[l]
    o = {}
    off = 0
    for name, size in (("gqkv", 3 * WM), ("gz", WM), ("gb", NH), ("ga", NH), ("fqkv", 3 * WM), ("ff", NH), ("nq", WM),
                       ("nkv", 6 * HEAD), ("ng", 3 * NH), ("mqkv", 3 * WM), ("mg", w.shape[1] - 3224)):
        o[name] = w[:, off:off + size]
        off += size
    small = jnp.concatenate([o["gb"], o["ga"], o["ff"], o["ng"]], 1)
    small = jnp.pad(small, ((0, 0), (0, LANE - small.shape[1])))
    w_r = jnp.concatenate([o["mg"], o["gqkv"], o["gz"], o["fqkv"], o["mqkv"], o["nq"], o["nkv"], small], 1)
    z1 = jnp.zeros((CMP_LEN, HEAD, HEAD), F32)
    w1k = nsa_cmp_k_w1[l].reshape(CMP_LEN, HEAD, HEAD)
    w1v = nsa_cmp_v_w1[l].reshape(CMP_LEN, HEAD, HEAD)
    w1_pair = jnp.concatenate([jnp.concatenate([w1k, z1], 2), jnp.concatenate([z1, w1v], 2)], 1).reshape(CMP_LEN * LANE, LANE)
    z2 = jnp.zeros((HEAD, HEAD), F32)
    w2_pair = jnp.concatenate([jnp.concatenate([nsa_cmp_k_w2[l], z2], 1), jnp.concatenate([z2, nsa_cmp_v_w2[l]], 1)], 0)
    pos_pair = jnp.concatenate([nsa_cmp_pos_k[l], nsa_cmp_pos_v[l]], 1).reshape(2, CMP_STRIDE * LANE)
    return w_r, pos_pair, w1_pair, w2_pair


def _rows_to_tiles(c_row, tk):
    nb, r, L = c_row.shape
    return c_row.reshape(nb, r, L // tk, tk).transpose(0, 2, 1, 3)


def kernel(x_prompt, x_sample, state_gdn_s, state_gdn_conv, cache_fox_k, cache_fox_v, cache_fox_logf, cache_nsa_ck, cache_nsa_cv, cache_nsa_sk, cache_nsa_sv, state_nsa_wk, state_nsa_wv, cache_moba_k, cache_moba_v, page_table, w_in, gdn_conv_w, gdn_a_log, gdn_dt_bias, gdn_norm_g, fox_f_bias, nsa_cmp_pos_k, nsa_cmp_k_w1, nsa_cmp_k_w2, nsa_cmp_pos_v, nsa_cmp_v_w1, nsa_cmp_v_w2, w_branch, w_out, ln1_g, ln1_b, w_up, w_down, ln2_g, ln2_b):
    B, T, D = x_prompt.shape
    Bs, Ts, _ = x_sample.shape
    depth = w_in.shape[0]
    n_pages = page_table.shape[1]
    P = n_pages * PAGE
    TP = SUB
    alpha = (2 * depth) ** 0.25
    n_pool = cache_fox_k.shape[0]
    PREC = 3

    xp = x_prompt.reshape(B * T, D)
    xs = jnp.pad(x_sample, ((0, 0), (0, TP - Ts), (0, 0))).reshape(Bs * TP, D)
    c_fk = cache_fox_k.reshape(n_pool, depth, PAGE, WM)
    c_fv = cache_fox_v.reshape(n_pool, depth, PAGE, WM)
    c_mk = cache_moba_k.reshape(n_pool, depth, PAGE, WM)
    c_mv = cache_moba_v.reshape(n_pool, depth, PAGE, WM)
    zeros_c0 = lambda nb: jnp.zeros((nb, SUB, LANE), F32)
    pad_heads = lambda a: jnp.pad(a, ((0, 0), (0, SUB - NH), (0, 0)))

    names = ("gdn_s", "gdn_conv", "fox_k", "fox_v", "fox_logf", "nsa_ck", "nsa_cv", "nsa_sk", "nsa_sv",
             "nsa_wk", "nsa_wv", "moba_k", "moba_v")
    acc_p = {n: [] for n in names}
    acc_s = {n: [] for n in names}

    for l in range(depth):
        w_r, pos_pair, w1_pair, w2_pair = _prep_layer(l, w_in, nsa_cmp_pos_k, nsa_cmp_k_w1, nsa_cmp_k_w2,
                                                      nsa_cmp_pos_v, nsa_cmp_v_w1, nsa_cmp_v_w2)
        f_bias = jnp.broadcast_to(jnp.pad(fox_f_bias[l], (0, SUB - NH))[:, None], (SUB, LANE))
        wb_bf, wo_bf = w_branch[l].astype(BF16), w_out[l].astype(BF16)
        wu_bf, wd_bf = w_up[l].astype(BF16), w_down[l].astype(BF16)

        hp = project(xp, w_r, PREC)
        h3 = hp.reshape(B, T, N_PROJ)
        o_a, s_new = gdn_branch(hp, B, T, T, gdn_conv_w[l], jnp.zeros((B, 3, 3 * WM), F32),
                                jnp.zeros((B, NH, HEAD, HEAD), F32), gdn_a_log[l], gdn_dt_bias[l], gdn_norm_g[l], PREC)
        f_row = pad_heads(jnp.swapaxes(h3[:, :, C_SM + S_FF:C_SM + S_FF + NH], 1, 2))
        logf_row, c_row = cum_logf(f_row, f_bias, zeros_c0(B), True)
        tk = min(T, 512)
        o_b = fox_attention(hp, hp, B, T, jnp.swapaxes(c_row[:, :NH], 1, 2), _rows_to_tiles(c_row, tk))
        nsa_c = h3[:, :, C_NC:C_NC + LANE].reshape(B * T, LANE)
        kcvc = nsa_compress(nsa_c, B, pos_pair, w1_pair, w2_pair)
        o_c = nsa_attention(hp, B, T, kcvc, [(hp, C_NS // LANE, 0, T)], [(hp, C_NW // LANE, 0, T)],
                            n_sel=-(-T // SEL_BLOCK))
        o_d = moba_attention(hp, B, T)
        x1 = mix_layer(xp, (o_a, o_b, o_c, o_d), hp, wb_bf, wo_bf, ln1_g[l], ln1_b[l], alpha)
        xp = mlp_layer(x1, wu_bf, wd_bf, ln2_g[l], ln2_b[l], alpha)

        keep = min(NSA_WIN, T)
        acc_p["gdn_s"].append(s_new)
        acc_p["gdn_conv"].append(h3[:, T - 3:, C_GQ:C_GQ + 3 * WM])
        acc_p["fox_k"].append(h3[:, :, C_FK:C_FK + WM].reshape(B, T, NH, HEAD))
        acc_p["fox_v"].append(h3[:, :, C_FV:C_FV + WM].reshape(B, T, NH, HEAD))
        acc_p["fox_logf"].append(jnp.swapaxes(logf_row[:, :NH], 1, 2))
        for i, n in enumerate(("nsa_ck", "nsa_cv", "nsa_sk", "nsa_sv")):
            acc_p[n].append(h3[:, :, C_NC + i * HEAD:C_NC + (i + 1) * HEAD])
        acc_p["nsa_wk"].append(h3[:, T - keep:, C_NW:C_NW + HEAD])
        acc_p["nsa_wv"].append(h3[:, T - keep:, C_NW + HEAD:C_NW + 2 * HEAD])
        acc_p["moba_k"].append(h3[:, :, C_MK:C_MK + WM].reshape(B, T, NH, HEAD))
        acc_p["moba_v"].append(h3[:, :, C_MV:C_MV + WM].reshape(B, T, NH, HEAD))

        hs = project(xs, w_r, PREC)
        h3s = hs.reshape(Bs, TP, N_PROJ)
        o_a, s_new = gdn_branch(hs, Bs, TP, Ts, gdn_conv_w[l], state_gdn_conv[:, l], state_gdn_s[:, l],
                                gdn_a_log[l], gdn_dt_bias[l], gdn_norm_g[l], PREC)
        past_fk = gather_pages([c_fk], page_table, l)
        past_fv = gather_pages([c_fv], page_table, l)
        past_lf = gather_pages([cache_fox_logf], page_table, l)
        _, c_past = cum_logf(pad_heads(jnp.swapaxes(past_lf, 1, 2)), f_bias, zeros_c0(Bs), False)
        f_new = pad_heads(jnp.swapaxes(h3s[:, :, C_SM + S_FF:C_SM + S_FF + NH], 1, 2))
        f_new = jnp.pad(f_new, ((0, 0), (0, 0), (0, LANE - TP)))
        c0 = jnp.broadcast_to(c_past[:, :, P - 1:P], (Bs, SUB, LANE))
        logf_new, c_new = cum_logf(f_new, f_bias, c0, True)
        tk = min(P, 1024)
        o_b = fox_attention(hs, None, Bs, TP, jnp.swapaxes(c_new[:, :NH, :TP], 1, 2), _rows_to_tiles(c_past, tk),
                            new=(past_fk, past_fv, c_new[:, :, :TP], P), qoff=P)
        past_c = gather_pages([cache_nsa_ck, cache_nsa_cv], page_table, l).reshape(Bs * P, LANE)
        past_s = gather_pages([cache_nsa_sk, cache_nsa_sv], page_table, l).reshape(Bs * P, LANE)
        kcvc = nsa_compress(past_c, Bs, pos_pair, w1_pair, w2_pair)
        n_buf = state_nsa_wk.shape[2]
        wbuf = jnp.concatenate([state_nsa_wk[:, l], state_nsa_wv[:, l]], -1).reshape(Bs * n_buf, LANE)
        o_c = nsa_attention(hs, Bs, TP, kcvc, [(past_s, 0, 0, P), (hs, C_NS // LANE, P, TP)],
                            [(wbuf, 0, P - n_buf, n_buf), (hs, C_NW // LANE, P, TP)],
                            n_sel=-(-(P + Ts) // SEL_BLOCK), qoff=P)
        past_mk = gather_pages([c_mk], page_table, l)
        past_mv = gather_pages([c_mv], page_table, l)
        o_d = moba_attention(hs, Bs, TP, new=(past_mk, past_mv, P), qoff=P)
        x1 = mix_layer(xs, (o_a, o_b, o_c, o_d), hs, wb_bf, wo_bf, ln1_g[l], ln1_b[l], alpha)
        xs = mlp_layer(x1, wu_bf, wd_bf, ln2_g[l], ln2_b[l], alpha)

        acc_s["gdn_s"].append(s_new)
        acc_s["gdn_conv"].append(h3s[:, Ts - 3:Ts, C_GQ:C_GQ + 3 * WM])
        acc_s["fox_k"].append(h3s[:, :Ts, C_FK:C_FK + WM].reshape(Bs, Ts, NH, HEAD))
        acc_s["fox_v"].append(h3s[:, :Ts, C_FV:C_FV + WM].reshape(Bs, Ts, NH, HEAD))
        acc_s["fox_logf"].append(jnp.swapaxes(logf_new[:, :NH, :Ts], 1, 2))
        for i, n in enumerate(("nsa_ck", "nsa_cv", "nsa_sk", "nsa_sv")):
            acc_s[n].append(h3s[:, :Ts, C_NC + i * HEAD:C_NC + (i + 1) * HEAD])
        kw = jnp.concatenate([state_nsa_wk[:, l], h3s[:, :Ts, C_NW:C_NW + HEAD]], 1)
        vw = jnp.concatenate([state_nsa_wv[:, l], h3s[:, :Ts, C_NW + HEAD:C_NW + 2 * HEAD]], 1)
        keep = min(NSA_WIN, kw.shape[1])
        acc_s["nsa_wk"].append(kw[:, kw.shape[1] - keep:])
        acc_s["nsa_wv"].append(vw[:, vw.shape[1] - keep:])
        acc_s["moba_k"].append(h3s[:, :Ts, C_MK:C_MK + WM].reshape(Bs, Ts, NH, HEAD))
        acc_s["moba_v"].append(h3s[:, :Ts, C_MV:C_MV + WM].reshape(Bs, Ts, NH, HEAD))

    outs = [xp.reshape(B, T, D), xs.reshape(Bs, TP, D)[:, :Ts]]
    for n in names:
        outs.append(jnp.stack(acc_p[n], 1))
        outs.append(jnp.stack(acc_s[n], 1))
    return tuple(outs)
```

```python
import functools
import math

import jax
import jax.numpy as jnp
from jax import lax
from jax.experimental import pallas as pl
from jax.experimental.pallas import tpu as pltpu

F32 = jnp.float32
BF16 = jnp.bfloat16
NEG = -1e30

HEAD = 64
NH = 4
WM = NH * HEAD
LANE = 128
SUB = 8
PAGE = 128
PAGES_PER_STEP = 8
GDN_CHUNK = 64
CMP_LEN, CMP_STRIDE = 32, 16
SEL_BLOCK, SEL_TOP, NSA_WIN = 64, 16, 512
NSA_FORCE = 1e4
MOBA_BLOCK, MOBA_TOP = 256, 3
LN_EPS = 1e-5
VMEM_LIMIT = 52 * 1024 * 1024

C_MG, C_GQ, C_GK, C_GV, C_GZ = 0, 4096, 4352, 4608, 4864
C_FQ, C_FK, C_FV = 5120, 5376, 5632
C_MQ, C_MK, C_MV = 5888, 6144, 6400
C_NQ, C_NC, C_NS, C_NW, C_SM = 6656, 6912, 7040, 7168, 7296
N_PROJ = 7424
S_GB, S_GA, S_FF, S_NG = 0, 4, 8, 12

SLOPES_NSA = tuple(2.0 ** (-float(i)) for i in (1, 3, 5, 7))
SLOPES_MOBA = tuple(2.0 ** (-float(i)) for i in (2, 4, 6, 8))


def _iota(shape, dim):
    return lax.broadcasted_iota(jnp.int32, shape, dim)


def _dg(a, b, nt=False, tn=False):
    if nt:
        dims = (((1,), (1,)), ((), ()))
    elif tn:
        dims = (((0,), (0,)), ((), ()))
    else:
        dims = (((1,), (0,)), ((), ()))
    return lax.dot_general(a, b, dims, preferred_element_type=F32)


def _split2(a):
    hi = a.astype(BF16)
    lo = (a - hi.astype(F32)).astype(BF16)
    return hi, lo


def _split3(a):
    a1 = a.astype(BF16)
    r = a - a1.astype(F32)
    a2 = r.astype(BF16)
    a3 = (r - a2.astype(F32)).astype(BF16)
    return a1, a2, a3


def mm(a, b, nt=False, tn=False, passes=3):
    if passes == 1:
        return _dg(a.astype(BF16), b.astype(BF16), nt, tn)
    ah, al = _split2(a)
    bh, bl = _split2(b)
    return _dg(ah, bh, nt, tn) + (_dg(ah, bl, nt, tn) + _dg(al, bh, nt, tn))


def mmx(a, b, nt=False, exact="b"):
    if exact == "b":
        bb = b.astype(BF16)
        a1, a2, a3 = _split3(a)
        return (_dg(a1, bb, nt) + _dg(a2, bb, nt)) + _dg(a3, bb, nt)
    aa = a.astype(BF16)
    b1, b2, b3 = _split3(b)
    return (_dg(aa, b1, nt) + _dg(aa, b2, nt)) + _dg(aa, b3, nt)


def _sigmoid(x):
    return 1.0 / (1.0 + jnp.exp(-x))


def _softplus(x):
    return jnp.maximum(x, 0.0) + jnp.log(1.0 + jnp.exp(-jnp.abs(x)))


def _layer_norm(x, g, b):
    mu = jnp.mean(x, -1, keepdims=True)
    xc = x - mu
    var = jnp.mean(xc * xc, -1, keepdims=True)
    return xc * lax.rsqrt(var + LN_EPS) * g + b


def _cparams(sem):
    return pltpu.CompilerParams(dimension_semantics=sem, vmem_limit_bytes=VMEM_LIMIT)


def _full_spec(shape):
    nd = len(shape)
    return pl.BlockSpec(shape, lambda *_: (0,) * nd)


def _proj_kernel(x_ref, w_ref, o_ref, *, n_gate_tiles):
    j = pl.program_id(1)

    @pl.when(j < n_gate_tiles)
    def _():
        o_ref[...] = mm(x_ref[...], w_ref[...], passes=1)

    @pl.when(j >= n_gate_tiles)
    def _():
        o_ref[...] = mm(x_ref[...], w_ref[...], passes=3)


def project(x, w):
    M, K = x.shape
    N = w.shape[1]
    tm = min(M, 1024)
    tn = 256
    return pl.pallas_call(
        functools.partial(_proj_kernel, n_gate_tiles=(C_GQ - C_MG) // tn),
        grid=(M // tm, N // tn),
        in_specs=[pl.BlockSpec((tm, K), lambda i, j: (i, 0)), pl.BlockSpec((K, tn), lambda i, j: (0, j))],
        out_specs=pl.BlockSpec((tm, tn), lambda i, j: (i, j)),
        out_shape=jax.ShapeDtypeStruct((M, N), F32),
        compiler_params=_cparams(("parallel", "arbitrary")),
        name="proj",
    )(x, w)


def _mix_kernel(x_ref, ba_ref, bb_ref, bc_ref, bd_ref, g_ref, wb_ref, wo_ref, lg_ref, lb_ref, o_ref, *, alpha):
    D = x_ref.shape[1]
    acc = None
    for n, br in enumerate((ba_ref, bb_ref, bc_ref, bd_ref)):
        up = _dg(br[...].astype(BF16), wb_ref[n])
        t = _sigmoid(g_ref[:, n * D:(n + 1) * D]) * up
        acc = t if acc is None else acc + t
    mixed = _dg(acc.astype(BF16), wo_ref[...])
    o_ref[...] = _layer_norm(alpha * x_ref[...] + mixed, lg_ref[...], lb_ref[...])


def mix_layer(x, branches, h, w_branch, w_out, ln_g, ln_b, alpha):
    M, D = x.shape
    tm = min(M, 256)
    row = lambda i: (i, 0)
    return pl.pallas_call(
        functools.partial(_mix_kernel, alpha=alpha),
        grid=(M // tm,),
        in_specs=[pl.BlockSpec((tm, D), row)] + [pl.BlockSpec((tm, WM), row)] * 4
        + [pl.BlockSpec((tm, NH * D), lambda i: (i, C_MG // (NH * D))),
           _full_spec(w_branch.shape), _full_spec(w_out.shape), _full_spec((1, D)), _full_spec((1, D))],
        out_specs=pl.BlockSpec((tm, D), row),
        out_shape=jax.ShapeDtypeStruct((M, D), F32),
        compiler_params=_cparams(("parallel",)),
        name="mix",
    )(x, *branches, h, w_branch, w_out, ln_g.reshape(1, D), ln_b.reshape(1, D))


def _mlp_kernel(x_ref, wu_ref, wd_ref, lg_ref, lb_ref, o_ref, acc_ref, *, alpha):
    f = pl.program_id(1)

    @pl.when(f == 0)
    def _():
        acc_ref[...] = jnp.zeros_like(acc_ref)

    hmid = jnp.square(jnp.maximum(_dg(x_ref[...].astype(BF16), wu_ref[...]), 0.0))
    acc_ref[...] += _dg(hmid.astype(BF16), wd_ref[...])

    @pl.when(f == pl.num_programs(1) - 1)
    def _():
        o_ref[...] = _layer_norm(alpha * x_ref[...] + acc_ref[...], lg_ref[...], lb_ref[...])


def mlp_layer(x, w_up, w_down, ln_g, ln_b, alpha):
    M, D = x.shape
    FF = w_up.shape[1]
    tm = min(M, 1024)
    tf = 512
    return pl.pallas_call(
        functools.partial(_mlp_kernel, alpha=alpha),
        grid=(M // tm, FF // tf),
        in_specs=[pl.BlockSpec((tm, D), lambda i, f: (i, 0)), pl.BlockSpec((D, tf), lambda i, f: (0, f)),
                  pl.BlockSpec((tf, D), lambda i, f: (f, 0)), _full_spec((1, D)), _full_spec((1, D))],
        out_specs=pl.BlockSpec((tm, D), lambda i, f: (i, 0)),
        out_shape=jax.ShapeDtypeStruct((M, D), F32),
        scratch_shapes=[pltpu.VMEM((tm, D), F32)],
        compiler_params=_cparams(("parallel", "arbitrary")),
        name="mlp",
    )(x, w_up, w_down, ln_g.reshape(1, D), ln_b.reshape(1, D))


def _cum_kernel(x_ref, bias_ref, c0_ref, lf_ref, c_ref, carry_ref, *, apply_ls):
    tb = x_ref.shape[2]

    @pl.when(pl.program_id(1) == 0)
    def _():
        carry_ref[...] = c0_ref[0]

    x = x_ref[0]
    if apply_ls:
        z = x + bias_ref[:, 0:1]
        x = jnp.minimum(z, 0.0) - jnp.log(1.0 + jnp.exp(-jnp.abs(z)))
    lf_ref[0] = x
    upper = (_iota((tb, tb), 0) <= _iota((tb, tb), 1)).astype(F32)
    cs = mmx(x, upper, exact="b") + carry_ref[:, 0:1]
    c_ref[0] = cs
    carry_ref[...] = jnp.broadcast_to(cs[:, tb - 1:tb], carry_ref.shape)


def cum_logf(x_row, bias, c0, apply_ls):
    B, R, T = x_row.shape
    tb = min(T, 512)
    return pl.pallas_call(
        functools.partial(_cum_kernel, apply_ls=apply_ls),
        grid=(B, T // tb),
        in_specs=[pl.BlockSpec((1, R, tb), lambda b, t: (b, 0, t)), _full_spec((R, LANE)),
                  pl.BlockSpec((1, R, LANE), lambda b, t: (b, 0, 0))],
        out_specs=[pl.BlockSpec((1, R, tb), lambda b, t: (b, 0, t))] * 2,
        out_shape=[jax.ShapeDtypeStruct((B, R, T), F32)] * 2,
        scratch_shapes=[pltpu.VMEM((R, LANE), F32)],
        compiler_params=_cparams(("parallel", "arbitrary")),
        name="cum_logf",
    )(x_row, bias, c0)


QK_PIECES = 1

def _q3(q):
    if QK_PIECES == 1:
        return q.astype(BF16)
    qh, ql = _split2(q)
    return jnp.concatenate([qh, ql, qh], axis=1)


def _k3(k, axis=1):
    if QK_PIECES == 1:
        return k.astype(BF16)
    kh, kl = _split2(k)
    return jnp.concatenate([kh, kh, kl], axis=axis)


def _attend(q3, sources, nrows, dv):
    carry = (jnp.full((nrows, 1), NEG, F32), jnp.zeros((nrows, 1), F32), jnp.zeros((nrows, dv), F32))
    for k3_tile, v_tile, bias, mask, j_lo, j_hi in sources:
        def body(j, carry, k3_tile=k3_tile, v_tile=v_tile, bias=bias, mask=mask):
            m, l, acc = carry
            s = _dg(q3, k3_tile(j), nt=True) + bias(j)
            if mask is not None:
                keep = mask(j)
                s = jnp.where(keep, s, NEG)
            m_new = jnp.maximum(m, jnp.max(s, -1, keepdims=True))
            alpha = jnp.exp(m - m_new)
            e = jnp.exp(s - m_new)
            if mask is not None:
                e = jnp.where(keep, e, 0.0)
            l = alpha * l + jnp.sum(e, -1, keepdims=True)
            acc = alpha * acc + _dg(e.astype(BF16), v_tile(j))
            return m_new, l, acc
        if isinstance(j_lo, int) and isinstance(j_hi, int) and j_hi - j_lo == 1:
            carry = body(j_lo, carry)
        else:
            carry = lax.fori_loop(j_lo, j_hi, body, carry)
    m, l, acc = carry
    return acc / jnp.maximum(l, 1e-30)


def _stack_heads_q(q, tq):
    keep = (_iota((NH * tq, 1), 0) // tq) == (_iota((1, WM), 1) // HEAD)
    return jnp.where(keep, jnp.concatenate([q] * NH, axis=0), 0.0)


def _unstack_heads(o, tq):
    lane_head = _iota((1, WM), 1) // HEAD
    out = o[0:tq]
    for h in range(1, NH):
        out = jnp.where(lane_head == h, o[h * tq:(h + 1) * tq], out)
    return out


def _per_head_rows(fn, tq):
    return jnp.concatenate([fn(h, slice(h * tq, (h + 1) * tq)) for h in range(NH)], axis=0)


def _rank_lt(score, n, top):
    lane = _iota(score.shape, 1)
    rank = jnp.zeros(score.shape, F32)
    for s2 in range(n):
        col = score[:, s2:s2 + 1]
        ahead = (col > score) | ((col == score) & (s2 < lane))
        rank = rank + ahead.astype(F32)
    return rank < top


def _kv_tiles(k_ref, v_ref, tk, scratch, first_step):
    n_main = k_ref.shape[0] // tk
    tile = lambda ref, j: ref[pl.ds(pl.multiple_of(j * tk, tk), tk), :]
    if scratch is None:
        return (lambda j: _k3(tile(k_ref, j))), (lambda j: tile(v_ref, j).astype(BF16))
    k3_ref, vb_ref = scratch

    @pl.when(first_step)
    def _():
        for j in range(n_main):
            rows = slice(j * tk, (j + 1) * tk)
            k3_ref[rows, :] = _k3(k_ref[rows, :])
            vb_ref[rows, :] = v_ref[rows, :].astype(BF16)

    return (lambda j: tile(k3_ref, j)), (lambda j: tile(vb_ref, j))


def _fox_kernel(*refs, tq, tk, qoff, new_off, cache_kv):
    has_new = new_off is not None
    refs = list(refs)
    q_ref, cq_ref, k_ref, v_ref, ck_ref = refs[:5]
    kn_ref, vn_ref, ckn_ref = refs[5:8] if has_new else (None, None, None)
    o_ref = refs[8 if has_new else 5]
    scratch = tuple(refs[-2:]) if cache_kv else None
    qb = pl.program_id(1)
    R = NH * tq
    n_main = k_ref.shape[0] // tk
    k3_tile, v_tile = _kv_tiles(k_ref, v_ref, tk, scratch, qb == 0)
    q0 = qoff + qb * tq
    qpos = q0 + _iota((R, 1), 0) % tq
    q3 = _q3(_stack_heads_q(q_ref[...] * (HEAD ** -0.5), tq))
    cq = jnp.concatenate([cq_ref[0, :, h:h + 1] for h in range(NH)], axis=0)

    def bias_main(j):
        ck = ck_ref[0, j]
        return _per_head_rows(lambda h, rows: cq[rows] - ck[h:h + 1, :], tq)

    mask_main = lambda j: (j * tk + _iota((1, tk), 1)) <= qpos
    j_full = jnp.minimum(n_main, q0 // tk)
    j_hi = jnp.minimum(n_main, (q0 + tq - 1) // tk + 1)
    srcs = [(k3_tile, v_tile, bias_main, None, 0, j_full), (k3_tile, v_tile, bias_main, mask_main, j_full, j_hi)]
    if has_new:
        bias_new = lambda j: _per_head_rows(lambda h, rows: cq[rows] - ckn_ref[0, h:h + 1, :], tq)
        mask_new = lambda j: (new_off + _iota((1, SUB), 1)) <= qpos
        srcs.append((lambda j: _k3(kn_ref[...]), lambda j: vn_ref[...].astype(BF16), bias_new, mask_new, 0, 1))
    o_ref[...] = _unstack_heads(_attend(q3, srcs, R, WM), tq)


def fox_attention(hq, hk, nb, T, cq_col, ck_main, new=None, qoff=0):
    tq = min(T, 128)
    nq = T // tq
    qspec = pl.BlockSpec((tq, WM), lambda b, i: (b * nq + i, C_FQ // WM))
    cqspec = pl.BlockSpec((1, tq, NH), lambda b, i: (b, i, 0))
    tk = ck_main.shape[-1]
    if new is None:
        L = T
        ins = [hq, cq_col, hk, hk, ck_main]
        specs = [qspec, cqspec, pl.BlockSpec((L, WM), lambda b, i: (b, C_FK // WM)),
                 pl.BlockSpec((L, WM), lambda b, i: (b, C_FV // WM)),
                 pl.BlockSpec((1, L // tk, SUB, tk), lambda b, i: (b, 0, 0, 0))]
        new_off = None
    else:
        k_main, v_main, ck_new, new_off = new
        L = k_main.shape[1]
        ins = [hq, cq_col, k_main.reshape(nb * L, WM), v_main.reshape(nb * L, WM), ck_main, hq, hq, ck_new]
        specs = [qspec, cqspec, pl.BlockSpec((L, WM), lambda b, i: (b, 0)), pl.BlockSpec((L, WM), lambda b, i: (b, 0)),
                 pl.BlockSpec((1, L // tk, SUB, tk), lambda b, i: (b, 0, 0, 0)),
                 pl.BlockSpec((tq, WM), lambda b, i: (b * nq + i, C_FK // WM)),
                 pl.BlockSpec((tq, WM), lambda b, i: (b * nq + i, C_FV // WM)),
                 pl.BlockSpec((1, SUB, SUB), lambda b, i: (b, 0, 0))]
        assert tq == SUB and nq == 1
    cache_kv = nq > 1
    scratch = [pltpu.VMEM((L, QK_PIECES * WM), BF16), pltpu.VMEM((L, WM), BF16)] if cache_kv else []
    return pl.pallas_call(
        functools.partial(_fox_kernel, tq=tq, tk=tk, qoff=qoff, new_off=new_off, cache_kv=cache_kv),
        grid=(nb, nq),
        in_specs=specs,
        out_specs=pl.BlockSpec((tq, WM), lambda b, i: (b * nq + i, 0)),
        out_shape=jax.ShapeDtypeStruct((nb * T, WM), F32),
        scratch_shapes=scratch,
        compiler_params=_cparams(("parallel", "arbitrary")),
        name="fox",
    )(*ins)


def _moba_kernel(*refs, tq, tk, qoff, new_off, cache_kv):
    has_new = new_off is not None
    refs = list(refs)
    q_ref, k_ref, v_ref = refs[:3]
    kn_ref, vn_ref = refs[3:5] if has_new else (None, None)
    o_ref, kmean_ref = refs[5:7] if has_new else refs[3:5]
    scratch = tuple(refs[-2:]) if cache_kv else None
    qb = pl.program_id(1)
    R = NH * tq
    L = k_ref.shape[0]
    nblk = L // MOBA_BLOCK
    n_main = L // tk
    blocks_per_tile = tk // MOBA_BLOCK

    @pl.when(qb == 0)
    def _():
        kmean_ref[...] = jnp.zeros_like(kmean_ref)
        for n in range(nblk):
            blk = k_ref[n * MOBA_BLOCK:(n + 1) * MOBA_BLOCK, :]
            kmean_ref[n:n + 1, :] = jnp.sum(blk, axis=0, keepdims=True) * (1.0 / MOBA_BLOCK)

    k3_tile, v_tile = _kv_tiles(k_ref, v_ref, tk, scratch, qb == 0)
    q0 = qoff + qb * tq
    qpos = q0 + _iota((R, 1), 0) % tq
    cur = qpos // MOBA_BLOCK
    qbd = _stack_heads_q(q_ref[...], tq)
    q3 = _q3(qbd * (HEAD ** -0.5))
    row_head = _iota((R, 1), 0) // tq
    slope = jnp.zeros((R, 1), F32)
    for h in range(NH):
        slope = jnp.where(row_head == h, SLOPES_MOBA[h], slope)
    col_q = -slope * qpos.astype(F32)

    blk_lane = _iota((1, LANE), 1)
    gs = mm(qbd, kmean_ref[...], nt=True)
    gs = jnp.where((blk_lane < cur) & (blk_lane < nblk), gs, NEG)
    sel = _rank_lt(gs, nblk, MOBA_TOP) & (gs > 0.5 * NEG)
    sel_bias = jnp.where(sel, 0.0, NEG)

    def alibi(kpos, col):
        kposf = kpos.astype(F32)
        return _per_head_rows(lambda h, rows: SLOPES_MOBA[h] * kposf + col[rows], tq)

    def bias_main(j):
        parts = []
        for i in range(blocks_per_tile):
            b = j * blocks_per_tile + i
            col_sel = jnp.sum(jnp.where(blk_lane == b, sel_bias, 0.0), axis=1, keepdims=True)
            parts.append(alibi(b * MOBA_BLOCK + _iota((1, MOBA_BLOCK), 1), col_sel + col_q))
        return parts[0] if blocks_per_tile == 1 else jnp.concatenate(parts, axis=1)

    srcs = [(k3_tile, v_tile, bias_main, None, 0, jnp.minimum(n_main, q0 // tk))]
    if has_new:
        kpos_own = new_off + _iota((1, SUB), 1)
        own = (lambda j: _k3(kn_ref[...]), lambda j: vn_ref[...].astype(BF16))
    else:
        assert blocks_per_tile == 1 and MOBA_BLOCK % tq == 0
        j_own = q0 // tk
        kpos_own = j_own * tk + _iota((1, tk), 1)
        own = (lambda j: k3_tile(j_own), lambda j: v_tile(j_own))
    mask_own = lambda j: ((kpos_own // MOBA_BLOCK) == cur) & (kpos_own <= qpos)
    srcs.append((own[0], own[1], lambda j: alibi(kpos_own, col_q), mask_own, 0, 1))
    o_ref[...] = _unstack_heads(_attend(q3, srcs, R, WM), tq)


def moba_attention(hq, nb, T, new=None, qoff=0):
    tq = min(T, 128)
    nq = T // tq
    qspec = pl.BlockSpec((tq, WM), lambda b, i: (b * nq + i, C_MQ // WM))
    if new is None:
        L = T
        ins = [hq, hq, hq]
        specs = [qspec, pl.BlockSpec((L, WM), lambda b, i: (b, C_MK // WM)),
                 pl.BlockSpec((L, WM), lambda b, i: (b, C_MV // WM))]
        new_off = None
    else:
        k_main, v_main, new_off = new
        L = k_main.shape[1]
        ins = [hq, k_main.reshape(nb * L, WM), v_main.reshape(nb * L, WM), hq, hq]
        specs = [qspec, pl.BlockSpec((L, WM), lambda b, i: (b, 0)), pl.BlockSpec((L, WM), lambda b, i: (b, 0)),
                 pl.BlockSpec((tq, WM), lambda b, i: (b * nq + i, C_MK // WM)),
                 pl.BlockSpec((tq, WM), lambda b, i: (b * nq + i, C_MV // WM))]
        assert tq == SUB and nq == 1
    tk = MOBA_BLOCK if new is None else min(L, 4 * MOBA_BLOCK)
    assert L // MOBA_BLOCK <= LANE
    cache_kv = nq > 1
    scratch = [pltpu.VMEM((L, QK_PIECES * WM), BF16), pltpu.VMEM((L, WM), BF16)] if cache_kv else []
    return pl.pallas_call(
        functools.partial(_moba_kernel, tq=tq, tk=tk, qoff=qoff, new_off=new_off, cache_kv=cache_kv),
        grid=(nb, nq),
        in_specs=specs,
        out_specs=pl.BlockSpec((tq, WM), lambda b, i: (b * nq + i, 0)),
        out_shape=jax.ShapeDtypeStruct((nb * T, WM), F32),
        scratch_shapes=[pltpu.VMEM((LANE, WM), F32)] + scratch,
        compiler_params=_cparams(("parallel", "arbitrary")),
        name="moba",
    )(*ins)


def _compress_kernel(g_ref, pos_ref, w1_ref, w2_ref, o_ref):
    half = g_ref.shape[1]
    ng = g_ref.shape[0]
    g = g_ref[...]
    a = mm(g + pos_ref[0:1, :], w1_ref[0:half, :])
    b = mm(g + pos_ref[1:2, :], w1_ref[half:2 * half, :])
    shift = (_iota((ng, ng), 1) == _iota((ng, ng), 0) + 1).astype(F32)
    pre = a + mmx(shift, b, exact="a")
    act = pre * _sigmoid(pre)
    o_ref[...] = mm(act, w2_ref[...])


def nsa_compress(rows_pair, nb, pos_pair, w1_pair, w2_pair):
    L = rows_pair.shape[0] // nb
    ng = L // CMP_STRIDE
    half = CMP_STRIDE * LANE
    g = rows_pair.reshape(nb * ng, half)
    return pl.pallas_call(
        _compress_kernel,
        grid=(nb,),
        in_specs=[pl.BlockSpec((ng, half), lambda b: (b, 0)), _full_spec(pos_pair.shape), _full_spec(w1_pair.shape),
                  _full_spec(w2_pair.shape)],
        out_specs=pl.BlockSpec((ng, LANE), lambda b: (b, 0)),
        out_shape=jax.ShapeDtypeStruct((nb * ng, LANE), F32),
        compiler_params=_cparams(("parallel",)),
        name="nsa_compress",
    )(g, pos_pair, w1_pair, w2_pair)


def _nsa_kernel(*refs, tq, tk, qoff, n_sel, sel_srcs, win_srcs, cache_kv):
    n_s, n_w = len(sel_srcs), len(win_srcs)
    q_ref, sm_ref, kc_ref = refs[:3]
    sel_refs = refs[3:3 + n_s]
    win_refs = refs[3 + n_s:3 + n_s + n_w]
    o_ref = refs[3 + n_s + n_w]
    scratch = refs[4 + n_s + n_w:] if cache_kv else (None, None)
    passes = 3
    qb = pl.program_id(1)
    R = NH * tq
    q0 = qoff + qb * tq
    qpos1 = q0 + _iota((tq, 1), 0)
    qpos = jnp.concatenate([qpos1] * NH, axis=0)
    row_head = _iota((R, 1), 0) // tq
    slope = jnp.zeros((R, 1), F32)
    for h in range(NH):
        slope = jnp.where(row_head == h, SLOPES_NSA[h], slope)
    lane = _iota((1, LANE), 1)
    scale = HEAD ** -0.5

    qs = []
    for h in range(NH):
        pair = q_ref[:, (h // 2) * LANE:(h // 2 + 1) * LANE]
        if h % 2:
            pair = pltpu.roll(pair, HEAD, axis=1)
        qs.append(jnp.where(lane < HEAD, pair, 0.0))
    q = jnp.concatenate(qs, axis=0)

    nc = kc_ref.shape[0]
    kcv = kc_ref[...]
    cend = _iota((1, nc), 1) * CMP_STRIDE + (CMP_LEN - 1)
    s_c = mm(q, kcv, nt=True, passes=passes) * scale - slope * (qpos - cend).astype(F32)
    m_c = cend <= qpos
    z = jnp.where(m_c, s_c, NEG)
    e = jnp.where(m_c, jnp.exp(z - jnp.max(z, -1, keepdims=True)), 0.0)
    p_c = e / jnp.maximum(jnp.sum(e, -1, keepdims=True), 1e-30)
    o_c = mm(p_c, kcv, passes=passes)

    n_sel_p = -(-n_sel // LANE) * LANE
    p_sum = p_c[0:tq]
    for h in range(1, NH):
        p_sum = p_sum + p_c[h * tq:(h + 1) * tq]
    cn = _iota((nc, n_sel_p), 0) * CMP_STRIDE
    sb = _iota((nc, n_sel_p), 1)
    overlap = ((cn < (sb + 1) * SEL_BLOCK) & (cn + (CMP_LEN - 1) >= sb * SEL_BLOCK)).astype(F32)
    imp = mmx(p_sum, overlap, exact="b")
    blk = _iota((1, n_sel_p), 1)
    cur = qpos1 // SEL_BLOCK
    valid = blk <= cur
    forced = (blk == 0) | (blk == cur) | (blk == cur - 1)
    score = jnp.where(valid, jnp.where(forced, NSA_FORCE, imp), NEG)
    sel1 = (_rank_lt(score, n_sel, SEL_TOP) & (score > 0.5 * NEG)).astype(F32)

    q3 = _q3(q * scale)
    col_q = -slope * qpos.astype(F32)

    def make_srcs(ref, k3_ref, pos0, nrows, kind):
        t = min(tk, nrows)
        n_t = nrows // t
        if k3_ref is None:
            raw = (lambda j: ref[...]) if n_t == 1 else (lambda j: ref[pl.ds(pl.multiple_of(j * t, t), t), :])
            k3_tile = lambda j: _k3(raw(j))
            v_tile = lambda j: raw(j).astype(BF16)
        else:
            @pl.when(qb == 0)
            def _():
                for j in range(n_t):
                    rows = slice(j * t, (j + 1) * t)
                    k3_ref[rows, :] = _k3(ref[rows, :])

            k3_tile = lambda j: k3_ref[pl.ds(pl.multiple_of(j * t, t), t), :]
            v_tile = lambda j: k3_ref[pl.ds(pl.multiple_of(j * t, t), t), 0:LANE]

        def kpos(j):
            return pos0 + j * t + _iota((1, t), 1)

        def alibi(j):
            kposf = kpos(j).astype(F32)
            return _per_head_rows(lambda h, rows: SLOPES_NSA[h] * kposf + col_q[rows], tq)

        def chosen(j):
            kb = (pos0 + j * t + _iota((n_sel_p, t), 1)) // SEL_BLOCK
            expand = (_iota((n_sel_p, t), 0) == kb).astype(BF16)
            return _dg(sel1.astype(BF16), expand)

        if kind == "win":
            def mask(j):
                dist = qpos - kpos(j)
                return (dist >= 0) & (dist < NSA_WIN) & (kpos(j) >= 0)
            if n_t == 1:
                return [(k3_tile, v_tile, alibi, mask, 0, 1)]
            lo = jnp.maximum(0, (q0 - (NSA_WIN - 1) - pos0) // t)
            hi = jnp.minimum(n_t, (q0 + tq - 1 - pos0) // t + 1)
            return [(k3_tile, v_tile, alibi, mask, lo, hi)]

        def mask(j):
            keep = jnp.concatenate([chosen(j) > 0.5] * NH, axis=0)
            return keep & (kpos(j) <= qpos)

        if n_t == 1:
            return [(k3_tile, v_tile, alibi, mask, 0, 1)]

        def bias_past(j):
            excl = (chosen(j) - 1.0) * (-NEG)
            kposf = kpos(j).astype(F32)
            return _per_head_rows(lambda h, rows: (excl + SLOPES_NSA[h] * kposf) + col_q[rows], tq)

        j_full = jnp.minimum(n_t, jnp.maximum(0, q0 - pos0) // t)
        j_hi = jnp.minimum(n_t, (q0 + tq - 1 - pos0) // t + 1)
        return [(k3_tile, v_tile, bias_past, None, 0, j_full), (k3_tile, v_tile, alibi, mask, j_full, j_hi)]

    sel_sources, win_sources = [], []
    for i, (r, (p0, n)) in enumerate(zip(sel_refs, sel_srcs)):
        sel_sources += make_srcs(r, scratch[0] if i == 0 else None, p0, n, "sel")
    for i, (r, (p0, n)) in enumerate(zip(win_refs, win_srcs)):
        win_sources += make_srcs(r, scratch[1] if i == 0 else None, p0, n, "win")
    o_g = _attend(q3, sel_sources, R, LANE)
    o_w = _attend(q3, win_sources, R, LANE)

    gates = _sigmoid(sm_ref[...])

    def gate_col(i):
        return jnp.concatenate([gates[:, S_NG + 3 * h + i:S_NG + 3 * h + i + 1] for h in range(NH)], axis=0)

    out = gate_col(0) * o_c + gate_col(1) * o_g + gate_col(2) * o_w
    for p in range(NH // 2):
        even = pltpu.roll(out[(2 * p) * tq:(2 * p + 1) * tq], HEAD, axis=1)
        odd = out[(2 * p + 1) * tq:(2 * p + 2) * tq]
        o_ref[:, p * LANE:(p + 1) * LANE] = jnp.where(lane < HEAD, even, odd)


def nsa_attention(hq, nb, T, kcvc, sel_list, win_list, n_sel, qoff=0):
    tq = min(T, 128)
    nq = T // tq
    nc = kcvc.shape[0] // nb
    ins = [hq, hq, kcvc]
    specs = [pl.BlockSpec((tq, WM), lambda b, i: (b * nq + i, C_NQ // WM)),
             pl.BlockSpec((tq, LANE), lambda b, i: (b * nq + i, C_SM // LANE)),
             pl.BlockSpec((nc, LANE), lambda b, i: (b, 0))]
    for arr, col, _, rows in list(sel_list) + list(win_list):
        ins.append(arr)
        specs.append(pl.BlockSpec((rows, LANE), lambda b, i, col=col: (b, col)))
    cache_kv = nq > 1
    kern = functools.partial(
        _nsa_kernel, tq=tq, tk=512 if cache_kv else 1024, qoff=qoff, n_sel=n_sel,
        sel_srcs=tuple((p0, rows) for _, _, p0, rows in sel_list),
        win_srcs=tuple((p0, rows) for _, _, p0, rows in win_list), cache_kv=cache_kv)
    scratch = ([pltpu.VMEM((sel_list[0][3], QK_PIECES * LANE), BF16), pltpu.VMEM((win_list[0][3], QK_PIECES * LANE), BF16)]
               if cache_kv else [])
    return pl.pallas_call(
        kern,
        grid=(nb, nq),
        in_specs=specs,
        out_specs=pl.BlockSpec((tq, WM), lambda b, i: (b * nq + i, 0)),
        out_shape=jax.ShapeDtypeStruct((nb * T, WM), F32),
        scratch_shapes=scratch,
        compiler_params=_cparams(("parallel", "arbitrary")),
        name="nsa",
    )(*ins)


def _gdn_kernel(q_ref, k_ref, v_ref, z_ref, sm_ref, cwq_ref, cwk_ref, cwv_ref, cbq_ref, cbk_ref, cbv_ref,
                s0_ref, par_ref, ng_ref, o_ref, sout_ref, xq_ref, xk_ref, xv_ref, s_ref, *, C, t_valid, passes):
    c = pl.program_id(1)
    R = NH * C
    bd = (_iota((WM, WM), 0) // HEAD) == (_iota((WM, WM), 1) // HEAD)
    bd_f = bd.astype(F32)

    @pl.when(c == 0)
    def _():
        xq_ref[0:SUB, :] = cbq_ref[0]
        xk_ref[0:SUB, :] = cbk_ref[0]
        xv_ref[0:SUB, :] = cbv_ref[0]
        spread = ((_iota((HEAD, WM), 1) % HEAD) == _iota((HEAD, WM), 0)).astype(F32)
        s_ref[...] = jnp.where(bd, mmx(s0_ref[0], spread, exact="b"), 0.0)

    def conv(x_ref, ext_ref, cw_ref):
        ext_ref[SUB:SUB + C, :] = x_ref[...]
        y = cw_ref[3:4, :] * ext_ref[SUB:SUB + C, :]
        for j in range(3):
            y = y + cw_ref[j:j + 1, :] * ext_ref[SUB - 3 + j:SUB - 3 + j + C, :]
        ext_ref[0:SUB, :] = ext_ref[C:C + SUB, :]
        return y * _sigmoid(y)

    qc = conv(q_ref, xq_ref, cwq_ref)
    kc = conv(k_ref, xk_ref, cwk_ref)
    vc = conv(v_ref, xv_ref, cwv_ref)
    qn = qc * lax.rsqrt(mmx(qc * qc, bd_f, exact="b") + 1e-6) * (HEAD ** -0.5)
    kn = kc * lax.rsqrt(mmx(kc * kc, bd_f, exact="b") + 1e-6)

    sm = sm_ref[...]
    tok_ok = (c * C + _iota((C, 1), 0)) < t_valid
    beta_t = jnp.where(tok_ok, _sigmoid(sm), 0.0)
    g_t = jnp.where(tok_ok, -jnp.exp(par_ref[0:1, :]) * _softplus(sm + par_ref[1:2, :]), 0.0)
    lower_c = (_iota((C, C), 1) <= _iota((C, C), 0)).astype(F32)
    gam_t = mmx(lower_c, g_t, exact="a")

    def stack_col(tile, lane0):
        return jnp.concatenate([tile[:, lane0 + h:lane0 + h + 1] for h in range(NH)], axis=0)

    bcol = stack_col(beta_t, S_GB)
    gcol = stack_col(gam_t, S_GA)
    gl_row = gam_t[C - 1:C, :]
    glcol = jnp.concatenate([jnp.broadcast_to(gl_row[:, S_GA + h:S_GA + h + 1], (C, 1)) for h in range(NH)], axis=0)
    gl_state = jnp.concatenate([jnp.broadcast_to(gl_row[:, S_GA + h:S_GA + h + 1], (HEAD, 1)) for h in range(NH)], axis=0)
    pick0 = (_iota((SUB, LANE), 1) == 0).astype(F32)
    grow = mmx(pick0, jnp.broadcast_to(gcol, (R, LANE)), nt=True, exact="a")[0:1, :]

    rr = _iota((R, R), 0)
    cc = _iota((R, R), 1)
    same = (rr // C) == (cc // C)
    lower = same & (cc <= rr)
    dec = jnp.where(lower, jnp.exp(jnp.where(lower, gcol - grow, 0.0)), 0.0)

    lane_head = _iota((1, WM), 1) // HEAD

    def stack_heads(x):
        return jnp.concatenate([jnp.where(lane_head == h, x, 0.0) for h in range(NH)], axis=0)

    kst, qst, vst = stack_heads(kn), stack_heads(qn), stack_heads(vc)
    a_mat = bcol * jnp.where(cc < rr, dec, 0.0) * mm(kst, kst, nt=True, passes=passes)
    t_inv = (rr == cc).astype(F32) - a_mat
    pw = a_mat
    for _ in range(int(math.log2(C)) - 1):
        pw = mm(pw, pw, passes=passes)
        t_inv = t_inv + mm(t_inv, pw, passes=passes)
    egam = jnp.exp(gcol)
    w_m = mm(t_inv, (bcol * egam) * kst, passes=passes)
    u_m = mm(t_inv, bcol * vst, passes=passes)
    p_m = dec * mm(qst, kst, nt=True, passes=passes)
    qg = egam * qst
    kd = jnp.exp(glcol - gcol) * kst

    s = s_ref[...]
    u2 = u_m - mm(w_m, s, passes=passes)
    o_st = mm(qg, s, passes=passes) + mm(p_m, u2, passes=passes)
    s_new = jnp.exp(gl_state) * s + mm(kd, u2, tn=True, passes=passes)
    s_ref[...] = s_new

    o = o_st[0:C]
    for h in range(1, NH):
        o = o + o_st[h * C:(h + 1) * C]
    o = o * lax.rsqrt(mmx(o * o, bd_f, exact="b") * (1.0 / HEAD) + 1e-6)
    zt = z_ref[...]
    o_ref[...] = o * ng_ref[...] * (zt * _sigmoid(zt))

    @pl.when(c == pl.num_programs(1) - 1)
    def _():
        gather = ((_iota((WM, HEAD), 0) % HEAD) == _iota((WM, HEAD), 1)).astype(F32)
        sout_ref[0] = mmx(s_new, gather, exact="b")


def gdn_branch(h, nb, T, t_valid, conv_w, conv_buf, s0, a_log, dt_bias, norm_g, passes=3):
    C = min(T, GDN_CHUNK)
    nchunk = T // C
    row = lambda col: pl.BlockSpec((C, WM), lambda b, c, col=col: (b * nchunk + c, col // WM))
    cb = jnp.pad(conv_buf, ((0, 0), (SUB - 3, 0), (0, 0)))
    par = jnp.zeros((SUB, LANE), F32).at[0, S_GA:S_GA + NH].set(a_log).at[1, S_GA:S_GA + NH].set(dt_bias)
    ng = jnp.tile(norm_g, NH).reshape(1, WM)
    cws = [conv_w[:, i * WM:(i + 1) * WM] for i in range(3)]
    cb_spec = lambda i: pl.BlockSpec((1, SUB, WM), lambda b, c, i=i: (b, 0, i))
    o, s_new = pl.pallas_call(
        functools.partial(_gdn_kernel, C=C, t_valid=t_valid, passes=passes),
        grid=(nb, nchunk),
        in_specs=[row(C_GQ), row(C_GK), row(C_GV), row(C_GZ),
                  pl.BlockSpec((C, LANE), lambda b, c: (b * nchunk + c, C_SM // LANE))]
        + [_full_spec((4, WM))] * 3 + [cb_spec(0), cb_spec(1), cb_spec(2),
                                        pl.BlockSpec((1, WM, HEAD), lambda b, c: (b, 0, 0)),
                                        _full_spec((SUB, LANE)), _full_spec((1, WM))],
        out_specs=[pl.BlockSpec((C, WM), lambda b, c: (b * nchunk + c, 0)),
                   pl.BlockSpec((1, WM, HEAD), lambda b, c: (b, 0, 0))],
        out_shape=[jax.ShapeDtypeStruct((nb * T, WM), F32), jax.ShapeDtypeStruct((nb, WM, HEAD), F32)],
        scratch_shapes=[pltpu.VMEM((SUB + C, WM), F32)] * 3 + [pltpu.VMEM((WM, WM), F32)],
        compiler_params=_cparams(("parallel", "arbitrary")),
        name="gdn",
    )(h, h, h, h, h, *cws, cb, cb, cb, s0.reshape(nb, WM, HEAD), par, ng)
    return o, s_new.reshape(nb, NH, HEAD, HEAD)


def key_minor(cache):
    nd = cache.ndim
    t = jnp.transpose(cache, (0, 1) + tuple(range(3, nd)) + (2,))
    return t.reshape(t.shape[0], t.shape[1], -1, t.shape[-1])


def _page_specs(cache_t, layer):
    G = PAGES_PER_STEP
    r = cache_t.shape[2]
    return [pl.BlockSpec((1, 1, r, PAGE), lambda b, i, pt, g=g: (pt[b, i * G + g], layer, 0, 0)) for g in range(G)]


def _paged_call(kern, page_table, layer, caches_t, others, other_specs, out_shape, out_spec, scratch, name):
    nb, n_pages = page_table.shape
    G = PAGES_PER_STEP
    ins, specs = list(others), list(other_specs)
    for c in caches_t:
        ins += [c] * G
        specs += _page_specs(c, layer)
    return pl.pallas_call(
        kern,
        grid_spec=pltpu.PrefetchScalarGridSpec(num_scalar_prefetch=1, grid=(nb, n_pages // G), in_specs=specs,
                                               out_specs=out_spec, scratch_shapes=scratch),
        out_shape=out_shape,
        compiler_params=_cparams(("parallel", "arbitrary")),
        name=name,
    )(page_table, *ins)


def _gather_logf_kernel(pt_ref, *refs):
    del pt_ref
    G = PAGES_PER_STEP
    o_ref = refs[G]
    o_ref[...] = jnp.zeros_like(o_ref)
    for g in range(G):
        o_ref[0, 0:NH, g * PAGE:(g + 1) * PAGE] = refs[g][0, 0]


def gather_logf(cache_t, page_table, layer):
    nb, n_pages = page_table.shape
    G = PAGES_PER_STEP
    return _paged_call(_gather_logf_kernel, page_table, layer, [cache_t], [], [],
                       jax.ShapeDtypeStruct((nb, SUB, n_pages * PAGE), F32),
                       pl.BlockSpec((1, SUB, G * PAGE), lambda b, i, pt: (b, 0, i)), [], "gather_logf")


def _gather_pair_kernel(pt_ref, *refs):
    del pt_ref
    G = PAGES_PER_STEP
    o_ref = refs[2 * G]
    ident = (_iota((PAGE, PAGE), 0) == _iota((PAGE, PAGE), 1)).astype(F32)
    for g in range(G):
        stacked = jnp.concatenate([refs[g][0, 0], refs[G + g][0, 0]], axis=0)
        o_ref[0, g * PAGE:(g + 1) * PAGE, :] = mmx(ident, stacked, nt=True, exact="a")


def gather_pair(cache_a_t, cache_b_t, page_table, layer):
    nb, n_pages = page_table.shape
    G = PAGES_PER_STEP
    return _paged_call(_gather_pair_kernel, page_table, layer, [cache_a_t, cache_b_t], [], [],
                       jax.ShapeDtypeStruct((nb, n_pages * PAGE, LANE), F32),
                       pl.BlockSpec((1, G * PAGE, LANE), lambda b, i, pt: (b, i, 0)), [], "gather_pair")


def _kmean_kernel(pt_ref, *refs):
    del pt_ref
    G = PAGES_PER_STEP
    o_ref = refs[G]
    j = pl.program_id(1)

    @pl.when(j == 0)
    def _():
        o_ref[...] = jnp.zeros_like(o_ref)

    acc = o_ref[0]
    for g in range(G):
        blk = (j * G + g) // (MOBA_BLOCK // PAGE)
        to_block = (_iota((PAGE, LANE), 1) == blk).astype(F32)
        acc = acc + mmx(refs[g][0, 0], to_block, exact="b") * (1.0 / MOBA_BLOCK)
    o_ref[0] = acc


def moba_block_means(cache_k_t, page_table, layer):
    nb, _ = page_table.shape
    return _paged_call(_kmean_kernel, page_table, layer, [cache_k_t], [], [],
                       jax.ShapeDtypeStruct((nb, WM, LANE), F32),
                       pl.BlockSpec((1, WM, LANE), lambda b, i, pt: (b, 0, 0)), [], "moba_block_means")


def _paged_attn_kernel(pt_ref, *refs, kind, tq, new_off):
    del pt_ref
    G = PAGES_PER_STEP
    R = NH * tq
    tk = G * PAGE
    nblk = new_off // MOBA_BLOCK
    n_in = 6 if kind == "fox" else 4
    q_ref, kn_ref, vn_ref = refs[:3]
    kt_refs = refs[n_in:n_in + G]
    vt_refs = refs[n_in + G:n_in + 2 * G]
    o_ref, m_ref, l_ref, acc_ref = refs[n_in + 2 * G:n_in + 2 * G + 4]
    j = pl.program_id(1)
    qpos = new_off + _iota((R, 1), 0) % tq
    kpos_new = new_off + _iota((1, SUB), 1)
    qbd = _stack_heads_q(q_ref[...], tq)
    q3 = _q3(qbd * (HEAD ** -0.5))

    if kind == "fox":
        cq_ref, ck_ref, ckn_ref = refs[3:6]
        cq = jnp.concatenate([cq_ref[0, :, h:h + 1] for h in range(NH)], axis=0)
        ck = ck_ref[0, 0]
        bias = _per_head_rows(lambda h, rows: cq[rows] - ck[h:h + 1, :], tq)
        bias_new = _per_head_rows(lambda h, rows: cq[rows] - ckn_ref[0, h:h + 1, :], tq)
        mask_new = kpos_new <= qpos
    else:
        km_ref = refs[3]
        selb_ref = refs[n_in + 2 * G + 4]
        cur = qpos // MOBA_BLOCK
        row_head = _iota((R, 1), 0) // tq
        slope = jnp.zeros((R, 1), F32)
        for h in range(NH):
            slope = jnp.where(row_head == h, SLOPES_MOBA[h], slope)
        col_q = -slope * qpos.astype(F32)
        blk_lane = _iota((1, LANE), 1)

        @pl.when(j == 0)
        def _():
            gs = mm(qbd, km_ref[0])
            gs = jnp.where((blk_lane < cur) & (blk_lane < nblk), gs, NEG)
            sel = _rank_lt(gs, nblk, MOBA_TOP) & (gs > 0.5 * NEG)
            selb_ref[...] = jnp.where(sel, 0.0, NEG)

        def alibi(kpos, col):
            kposf = kpos.astype(F32)
            return _per_head_rows(lambda h, rows: SLOPES_MOBA[h] * kposf + col[rows], tq)

        parts = []
        for i in range(tk // MOBA_BLOCK):
            b = j * (tk // MOBA_BLOCK) + i
            col_sel = jnp.sum(jnp.where(blk_lane == b, selb_ref[...], 0.0), axis=1, keepdims=True)
            parts.append(alibi(b * MOBA_BLOCK + _iota((1, MOBA_BLOCK), 1), col_sel + col_q))
        bias = jnp.concatenate(parts, axis=1)
        bias_new = alibi(kpos_new, col_q)
        mask_new = ((kpos_new // MOBA_BLOCK) == cur) & (kpos_new <= qpos)

    @pl.when(j == 0)
    def _():
        m_ref[...] = jnp.full(m_ref.shape, NEG, F32)
        l_ref[...] = jnp.zeros_like(l_ref)
        acc_ref[...] = jnp.zeros_like(acc_ref)

    def update(carry, s, pv, keep):
        m, l, acc = carry
        if keep is not None:
            s = jnp.where(keep, s, NEG)
        m_new = jnp.maximum(m, jnp.max(s, -1, keepdims=True))
        alpha = jnp.exp(m - m_new)
        e = jnp.exp(s - m_new)
        if keep is not None:
            e = jnp.where(keep, e, 0.0)
        return m_new, alpha * l + jnp.sum(e, -1, keepdims=True), alpha * acc + pv(e.astype(BF16))

    kt = jnp.concatenate([r[0, 0] for r in kt_refs], axis=1)
    vt = jnp.concatenate([r[0, 0] for r in vt_refs], axis=1).astype(BF16)
    s = _dg(q3, _k3(kt, axis=0)) + bias
    carry = update((m_ref[:, 0:1], l_ref[:, 0:1], acc_ref[...]), s, lambda e: _dg(e, vt, nt=True), None)
    m_ref[...] = jnp.broadcast_to(carry[0], m_ref.shape)
    l_ref[...] = jnp.broadcast_to(carry[1], l_ref.shape)
    acc_ref[...] = carry[2]

    @pl.when(j == pl.num_programs(1) - 1)
    def _():
        vn = vn_ref[...].astype(BF16)
        s_new = _dg(q3, _k3(kn_ref[...]), nt=True) + bias_new
        _, l, acc = update(carry, s_new, lambda e: _dg(e, vn), mask_new)
        o_ref[...] = _unstack_heads(acc / jnp.maximum(l, 1e-30), tq)


def paged_attention(kind, hs, nb, tq, page_table, layer, cache_k_t, cache_v_t, extras, new_off):
    G = PAGES_PER_STEP
    R = NH * tq
    c_q, c_k, c_v = (C_FQ, C_FK, C_FV) if kind == "fox" else (C_MQ, C_MK, C_MV)
    row = lambda col: pl.BlockSpec((tq, WM), lambda b, i, pt, col=col: (b, col // WM))
    others = [hs, hs, hs] + list(extras)
    specs = [row(c_q), row(c_k), row(c_v)]
    if kind == "fox":
        tk = extras[1].shape[-1]
        assert tk == G * PAGE
        specs += [pl.BlockSpec((1, tq, NH), lambda b, i, pt: (b, 0, 0)),
                  pl.BlockSpec((1, 1, SUB, tk), lambda b, i, pt: (b, i, 0, 0)),
                  pl.BlockSpec((1, SUB, SUB), lambda b, i, pt: (b, 0, 0))]
    else:
        specs += [pl.BlockSpec((1, WM, LANE), lambda b, i, pt: (b, 0, 0))]
    scratch = [pltpu.VMEM((R, LANE), F32), pltpu.VMEM((R, LANE), F32), pltpu.VMEM((R, WM), F32)]
    if kind == "moba":
        scratch.append(pltpu.VMEM((R, LANE), F32))
    return _paged_call(functools.partial(_paged_attn_kernel, kind=kind, tq=tq, new_off=new_off), page_table, layer,
                       [cache_k_t, cache_v_t], others, specs, jax.ShapeDtypeStruct((nb * tq, WM), F32),
                       pl.BlockSpec((tq, WM), lambda b, i, pt: (b, 0)), scratch, "paged_" + kind)


def _prep_layer(l, w_in, nsa_cmp_pos_k, nsa_cmp_k_w1, nsa_cmp_k_w2, nsa_cmp_pos_v, nsa_cmp_v_w1, nsa_cmp_v_w2):
    w = w_in[l]
    o = {}
    off = 0
    for name, size in (("gqkv", 3 * WM), ("gz", WM), ("gb", NH), ("ga", NH), ("fqkv", 3 * WM), ("ff", NH), ("nq", WM),
                       ("nkv", 6 * HEAD), ("ng", 3 * NH), ("mqkv", 3 * WM), ("mg", w.shape[1] - 3224)):
        o[name] = w[:, off:off + size]
        off += size
    small = jnp.concatenate([o["gb"], o["ga"], o["ff"], o["ng"]], 1)
    small = jnp.pad(small, ((0, 0), (0, LANE - small.shape[1])))
    w_r = jnp.concatenate([o["mg"], o["gqkv"], o["gz"], o["fqkv"], o["mqkv"], o["nq"], o["nkv"], small], 1)
    z1 = jnp.zeros((CMP_LEN, HEAD, HEAD), F32)
    w1k = nsa_cmp_k_w1[l].reshape(CMP_LEN, HEAD, HEAD)
    w1v = nsa_cmp_v_w1[l].reshape(CMP_LEN, HEAD, HEAD)
    w1_pair = jnp.concatenate([jnp.concatenate([w1k, z1], 2), jnp.concatenate([z1, w1v], 2)], 1).reshape(CMP_LEN * LANE, LANE)
    z2 = jnp.zeros((HEAD, HEAD), F32)
    w2_pair = jnp.concatenate([jnp.concatenate([nsa_cmp_k_w2[l], z2], 1), jnp.concatenate([z2, nsa_cmp_v_w2[l]], 1)], 0)
    pos_pair = jnp.concatenate([nsa_cmp_pos_k[l], nsa_cmp_pos_v[l]], 1).reshape(2, CMP_STRIDE * LANE)
    return w_r, pos_pair, w1_pair, w2_pair


def _rows_to_tiles(c_row, tk):
    nb, r, L = c_row.shape
    return c_row.reshape(nb, r, L // tk, tk).transpose(0, 2, 1, 3)


def kernel(x_prompt, x_sample, state_gdn_s, state_gdn_conv, cache_fox_k, cache_fox_v, cache_fox_logf, cache_nsa_ck, cache_nsa_cv, cache_nsa_sk, cache_nsa_sv, state_nsa_wk, state_nsa_wv, cache_moba_k, cache_moba_v, page_table, w_in, gdn_conv_w, gdn_a_log, gdn_dt_bias, gdn_norm_g, fox_f_bias, nsa_cmp_pos_k, nsa_cmp_k_w1, nsa_cmp_k_w2, nsa_cmp_pos_v, nsa_cmp_v_w1, nsa_cmp_v_w2, w_branch, w_out, ln1_g, ln1_b, w_up, w_down, ln2_g, ln2_b):
    B, T, D = x_prompt.shape
    Bs, Ts, _ = x_sample.shape
    depth = w_in.shape[0]
    n_pages = page_table.shape[1]
    P = n_pages * PAGE
    TP = SUB
    alpha = (2 * depth) ** 0.25
    n_pool = cache_fox_k.shape[0]
    PREC = 3

    xp = x_prompt.reshape(B * T, D)
    xs = jnp.pad(x_sample, ((0, 0), (0, TP - Ts), (0, 0))).reshape(Bs * TP, D)
    fk_t, fv_t, lf_t = key_minor(cache_fox_k), key_minor(cache_fox_v), key_minor(cache_fox_logf)
    mk_t, mv_t = key_minor(cache_moba_k), key_minor(cache_moba_v)
    nck_t, ncv_t = key_minor(cache_nsa_ck), key_minor(cache_nsa_cv)
    nsk_t, nsv_t = key_minor(cache_nsa_sk), key_minor(cache_nsa_sv)
    zeros_c0 = lambda nb: jnp.zeros((nb, SUB, LANE), F32)
    pad_heads = lambda a: jnp.pad(a, ((0, 0), (0, SUB - NH), (0, 0)))

    names = ("gdn_s", "gdn_conv", "fox_k", "fox_v", "fox_logf", "nsa_ck", "nsa_cv", "nsa_sk", "nsa_sv",
             "nsa_wk", "nsa_wv", "moba_k", "moba_v")
    acc_p = {n: [] for n in names}
    acc_s = {n: [] for n in names}

    for l in range(depth):
        w_r, pos_pair, w1_pair, w2_pair = _prep_layer(l, w_in, nsa_cmp_pos_k, nsa_cmp_k_w1, nsa_cmp_k_w2,
                                                      nsa_cmp_pos_v, nsa_cmp_v_w1, nsa_cmp_v_w2)
        f_bias = jnp.broadcast_to(jnp.pad(fox_f_bias[l], (0, SUB - NH))[:, None], (SUB, LANE))
        wb_bf, wo_bf = w_branch[l].astype(BF16), w_out[l].astype(BF16)
        wu_bf, wd_bf = w_up[l].astype(BF16), w_down[l].astype(BF16)

        hp = project(xp, w_r)
        h3 = hp.reshape(B, T, N_PROJ)
        o_a, s_new = gdn_branch(hp, B, T, T, gdn_conv_w[l], jnp.zeros((B, 3, 3 * WM), F32),
                                jnp.zeros((B, NH, HEAD, HEAD), F32), gdn_a_log[l], gdn_dt_bias[l], gdn_norm_g[l], PREC)
        f_row = pad_heads(jnp.swapaxes(h3[:, :, C_SM + S_FF:C_SM + S_FF + NH], 1, 2))
        logf_row, c_row = cum_logf(f_row, f_bias, zeros_c0(B), True)
        tk = min(T, 512)
        o_b = fox_attention(hp, hp, B, T, jnp.swapaxes(c_row[:, :NH], 1, 2), _rows_to_tiles(c_row, tk))
        nsa_c = h3[:, :, C_NC:C_NC + LANE].reshape(B * T, LANE)
        kcvc = nsa_compress(nsa_c, B, pos_pair, w1_pair, w2_pair)
        o_c = nsa_attention(hp, B, T, kcvc, [(hp, C_NS // LANE, 0, T)], [(hp, C_NW // LANE, 0, T)],
                            n_sel=-(-T // SEL_BLOCK))
        o_d = moba_attention(hp, B, T)
        x1 = mix_layer(xp, (o_a, o_b, o_c, o_d), hp, wb_bf, wo_bf, ln1_g[l], ln1_b[l], alpha)
        xp = mlp_layer(x1, wu_bf, wd_bf, ln2_g[l], ln2_b[l], alpha)

        keep = min(NSA_WIN, T)
        acc_p["gdn_s"].append(s_new)
        acc_p["gdn_conv"].append(h3[:, T - 3:, C_GQ:C_GQ + 3 * WM])
        acc_p["fox_k"].append(h3[:, :, C_FK:C_FK + WM].reshape(B, T, NH, HEAD))
        acc_p["fox_v"].append(h3[:, :, C_FV:C_FV + WM].reshape(B, T, NH, HEAD))
        acc_p["fox_logf"].append(jnp.swapaxes(logf_row[:, :NH], 1, 2))
        for i, n in enumerate(("nsa_ck", "nsa_cv", "nsa_sk", "nsa_sv")):
            acc_p[n].append(h3[:, :, C_NC + i * HEAD:C_NC + (i + 1) * HEAD])
        acc_p["nsa_wk"].append(h3[:, T - keep:, C_NW:C_NW + HEAD])
        acc_p["nsa_wv"].append(h3[:, T - keep:, C_NW + HEAD:C_NW + 2 * HEAD])
        acc_p["moba_k"].append(h3[:, :, C_MK:C_MK + WM].reshape(B, T, NH, HEAD))
        acc_p["moba_v"].append(h3[:, :, C_MV:C_MV + WM].reshape(B, T, NH, HEAD))

        hs = project(xs, w_r)
        h3s = hs.reshape(Bs, TP, N_PROJ)
        o_a, s_new = gdn_branch(hs, Bs, TP, Ts, gdn_conv_w[l], state_gdn_conv[:, l], state_gdn_s[:, l],
                                gdn_a_log[l], gdn_dt_bias[l], gdn_norm_g[l], PREC)
        _, c_past = cum_logf(gather_logf(lf_t, page_table, l), f_bias, zeros_c0(Bs), False)
        f_new = pad_heads(jnp.swapaxes(h3s[:, :, C_SM + S_FF:C_SM + S_FF + NH], 1, 2))
        f_new = jnp.pad(f_new, ((0, 0), (0, 0), (0, LANE - TP)))
        c0 = jnp.broadcast_to(c_past[:, :, P - 1:P], (Bs, SUB, LANE))
        logf_new, c_new = cum_logf(f_new, f_bias, c0, True)
        o_b = paged_attention("fox", hs, Bs, TP, page_table, l, fk_t, fv_t,
                              (jnp.swapaxes(c_new[:, :NH, :TP], 1, 2), _rows_to_tiles(c_past, PAGES_PER_STEP * PAGE),
                               c_new[:, :, :TP]), P)
        past_c = gather_pair(nck_t, ncv_t, page_table, l).reshape(Bs * P, LANE)
        past_s = gather_pair(nsk_t, nsv_t, page_table, l).reshape(Bs * P, LANE)
        kcvc = nsa_compress(past_c, Bs, pos_pair, w1_pair, w2_pair)
        n_buf = state_nsa_wk.shape[2]
        wbuf = jnp.concatenate([state_nsa_wk[:, l], state_nsa_wv[:, l]], -1).reshape(Bs * n_buf, LANE)
        o_c = nsa_attention(hs, Bs, TP, kcvc, [(past_s, 0, 0, P), (hs, C_NS // LANE, P, TP)],
                            [(wbuf, 0, P - n_buf, n_buf), (hs, C_NW // LANE, P, TP)],
                            n_sel=-(-(P + Ts) // SEL_BLOCK), qoff=P)
        o_d = paged_attention("moba", hs, Bs, TP, page_table, l, mk_t, mv_t,
                              (moba_block_means(mk_t, page_table, l),), P)
        x1 = mix_layer(xs, (o_a, o_b, o_c, o_d), hs, wb_bf, wo_bf, ln1_g[l], ln1_b[l], alpha)
        xs = mlp_layer(x1, wu_bf, wd_bf, ln2_g[l], ln2_b[l], alpha)

        acc_s["gdn_s"].append(s_new)
        acc_s["gdn_conv"].append(h3s[:, Ts - 3:Ts, C_GQ:C_GQ + 3 * WM])
        acc_s["fox_k"].append(h3s[:, :Ts, C_FK:C_FK + WM].reshape(Bs, Ts, NH, HEAD))
        acc_s["fox_v"].append(h3s[:, :Ts, C_FV:C_FV + WM].reshape(Bs, Ts, NH, HEAD))
        acc_s["fox_logf"].append(jnp.swapaxes(logf_new[:, :NH, :Ts], 1, 2))
        for i, n in enumerate(("nsa_ck", "nsa_cv", "nsa_sk", "nsa_sv")):
            acc_s[n].append(h3s[:, :Ts, C_NC + i * HEAD:C_NC + (i + 1) * HEAD])
        kw = jnp.concatenate([state_nsa_wk[:, l], h3s[:, :Ts, C_NW:C_NW + HEAD]], 1)
        vw = jnp.concatenate([state_nsa_wv[:, l], h3s[:, :Ts, C_NW + HEAD:C_NW + 2 * HEAD]], 1)
        keep = min(NSA_WIN, kw.shape[1])
        acc_s["nsa_wk"].append(kw[:, kw.shape[1] - keep:])
        acc_s["nsa_wv"].append(vw[:, vw.shape[1] - keep:])
        acc_s["moba_k"].append(h3s[:, :Ts, C_MK:C_MK + WM].reshape(Bs, Ts, NH, HEAD))
        acc_s["moba_v"].append(h3s[:, :Ts, C_MV:C_MV + WM].reshape(Bs, Ts, NH, HEAD))

    outs = [xp.reshape(B, T, D), xs.reshape(Bs, TP, D)[:, :Ts]]
    for n in names:
        outs.append(jnp.stack(acc_p[n], 1))
        outs.append(jnp.stack(acc_s[n], 1))
    return tuple(outs)
```

```python
import functools
import math

import jax
import jax.numpy as jnp
from jax import lax
from jax.experimental import pallas as pl
from jax.experimental.pallas import tpu as pltpu

F32 = jnp.float32
BF16 = jnp.bfloat16
NEG = -1e30

HEAD = 64
NH = 4
WM = NH * HEAD
LANE = 128
SUB = 8
PAGE = 128
PAGES_PER_STEP = 16
GDN_CHUNK = 64
CMP_LEN, CMP_STRIDE = 32, 16
SEL_BLOCK, SEL_TOP, NSA_WIN = 64, 16, 512
NSA_FORCE = 1e4
MOBA_BLOCK, MOBA_TOP = 256, 3
LN_EPS = 1e-5
VMEM_LIMIT = 52 * 1024 * 1024

C_MG, C_GQ, C_GK, C_GV, C_GZ = 0, 4096, 4352, 4608, 4864
C_FQ, C_FK, C_FV = 5120, 5376, 5632
C_MQ, C_MK, C_MV = 5888, 6144, 6400
C_NQ, C_NC, C_NS, C_NW, C_SM = 6656, 6912, 7040, 7168, 7296
N_PROJ = 7424
S_GB, S_GA, S_FF, S_NG = 0, 4, 8, 12

SLOPES_NSA = tuple(2.0 ** (-float(i)) for i in (1, 3, 5, 7))
SLOPES_MOBA = tuple(2.0 ** (-float(i)) for i in (2, 4, 6, 8))


def _iota(shape, dim):
    return lax.broadcasted_iota(jnp.int32, shape, dim)


def _dg(a, b, nt=False, tn=False):
    if nt:
        dims = (((1,), (1,)), ((), ()))
    elif tn:
        dims = (((0,), (0,)), ((), ()))
    else:
        dims = (((1,), (0,)), ((), ()))
    return lax.dot_general(a, b, dims, preferred_element_type=F32)


def _split2(a):
    hi = a.astype(BF16)
    lo = (a - hi.astype(F32)).astype(BF16)
    return hi, lo


def _split3(a):
    a1 = a.astype(BF16)
    r = a - a1.astype(F32)
    a2 = r.astype(BF16)
    a3 = (r - a2.astype(F32)).astype(BF16)
    return a1, a2, a3


def mm(a, b, nt=False, tn=False, passes=3):
    if passes == 1:
        return _dg(a.astype(BF16), b.astype(BF16), nt, tn)
    ah, al = _split2(a)
    bh, bl = _split2(b)
    return _dg(ah, bh, nt, tn) + (_dg(ah, bl, nt, tn) + _dg(al, bh, nt, tn))


def mmx(a, b, nt=False, exact="b"):
    if exact == "b":
        bb = b.astype(BF16)
        a1, a2, a3 = _split3(a)
        return (_dg(a1, bb, nt) + _dg(a2, bb, nt)) + _dg(a3, bb, nt)
    aa = a.astype(BF16)
    b1, b2, b3 = _split3(b)
    return (_dg(aa, b1, nt) + _dg(aa, b2, nt)) + _dg(aa, b3, nt)


def _sigmoid(x):
    return 1.0 / (1.0 + jnp.exp(-x))


def _softplus(x):
    return jnp.maximum(x, 0.0) + jnp.log(1.0 + jnp.exp(-jnp.abs(x)))


def _layer_norm(x, g, b):
    mu = jnp.mean(x, -1, keepdims=True)
    xc = x - mu
    var = jnp.mean(xc * xc, -1, keepdims=True)
    return xc * lax.rsqrt(var + LN_EPS) * g + b


def _cparams(sem):
    return pltpu.CompilerParams(dimension_semantics=sem, vmem_limit_bytes=VMEM_LIMIT)


def _full_spec(shape):
    nd = len(shape)
    return pl.BlockSpec(shape, lambda *_: (0,) * nd)


def _proj_kernel(x_ref, w_ref, o_ref, xh_ref, xl_ref, *, n_gate_tiles):
    j = pl.program_id(1)

    @pl.when(j == 0)
    def _():
        hi, lo = _split2(x_ref[...])
        xh_ref[...] = hi
        xl_ref[...] = lo

    @pl.when(j < n_gate_tiles)
    def _():
        o_ref[...] = _dg(xh_ref[...], w_ref[...].astype(BF16))

    @pl.when(j >= n_gate_tiles)
    def _():
        wh, wl = _split2(w_ref[...])
        xh = xh_ref[...]
        o_ref[...] = _dg(xh, wh) + (_dg(xh, wl) + _dg(xl_ref[...], wh))


def project(x, w):
    M, K = x.shape
    N = w.shape[1]
    tm = min(M, 1024)
    tn = 256
    return pl.pallas_call(
        functools.partial(_proj_kernel, n_gate_tiles=(C_GQ - C_MG) // tn),
        grid=(M // tm, N // tn),
        in_specs=[pl.BlockSpec((tm, K), lambda i, j: (i, 0)), pl.BlockSpec((K, tn), lambda i, j: (0, j))],
        out_specs=pl.BlockSpec((tm, tn), lambda i, j: (i, j)),
        out_shape=jax.ShapeDtypeStruct((M, N), F32),
        scratch_shapes=[pltpu.VMEM((tm, K), BF16), pltpu.VMEM((tm, K), BF16)],
        compiler_params=_cparams(("parallel", "arbitrary")),
        name="proj",
    )(x, w)


def _mix_kernel(x_ref, ba_ref, bb_ref, bc_ref, bd_ref, g_ref, wb_ref, wo_ref, lg_ref, lb_ref, o_ref, *, alpha):
    D = x_ref.shape[1]
    acc = None
    for n, br in enumerate((ba_ref, bb_ref, bc_ref, bd_ref)):
        up = _dg(br[...].astype(BF16), wb_ref[n])
        t = _sigmoid(g_ref[:, n * D:(n + 1) * D]) * up
        acc = t if acc is None else acc + t
    mixed = _dg(acc.astype(BF16), wo_ref[...])
    o_ref[...] = _layer_norm(alpha * x_ref[...] + mixed, lg_ref[...], lb_ref[...])


def mix_layer(x, branches, h, w_branch, w_out, ln_g, ln_b, alpha):
    M, D = x.shape
    tm = min(M, 256)
    row = lambda i: (i, 0)
    return pl.pallas_call(
        functools.partial(_mix_kernel, alpha=alpha),
        grid=(M // tm,),
        in_specs=[pl.BlockSpec((tm, D), row)] + [pl.BlockSpec((tm, WM), row)] * 4
        + [pl.BlockSpec((tm, NH * D), lambda i: (i, C_MG // (NH * D))),
           _full_spec(w_branch.shape), _full_spec(w_out.shape), _full_spec((1, D)), _full_spec((1, D))],
        out_specs=pl.BlockSpec((tm, D), row),
        out_shape=jax.ShapeDtypeStruct((M, D), F32),
        compiler_params=_cparams(("parallel",)),
        name="mix",
    )(x, *branches, h, w_branch, w_out, ln_g.reshape(1, D), ln_b.reshape(1, D))


def _mlp_kernel(x_ref, wu_ref, wd_ref, lg_ref, lb_ref, o_ref, acc_ref, *, alpha):
    f = pl.program_id(1)

    @pl.when(f == 0)
    def _():
        acc_ref[...] = jnp.zeros_like(acc_ref)

    hmid = jnp.square(jnp.maximum(_dg(x_ref[...].astype(BF16), wu_ref[...]), 0.0))
    acc_ref[...] += _dg(hmid.astype(BF16), wd_ref[...])

    @pl.when(f == pl.num_programs(1) - 1)
    def _():
        o_ref[...] = _layer_norm(alpha * x_ref[...] + acc_ref[...], lg_ref[...], lb_ref[...])


def mlp_layer(x, w_up, w_down, ln_g, ln_b, alpha):
    M, D = x.shape
    FF = w_up.shape[1]
    tm = min(M, 1024)
    tf = 512
    return pl.pallas_call(
        functools.partial(_mlp_kernel, alpha=alpha),
        grid=(M // tm, FF // tf),
        in_specs=[pl.BlockSpec((tm, D), lambda i, f: (i, 0)), pl.BlockSpec((D, tf), lambda i, f: (0, f)),
                  pl.BlockSpec((tf, D), lambda i, f: (f, 0)), _full_spec((1, D)), _full_spec((1, D))],
        out_specs=pl.BlockSpec((tm, D), lambda i, f: (i, 0)),
        out_shape=jax.ShapeDtypeStruct((M, D), F32),
        scratch_shapes=[pltpu.VMEM((tm, D), F32)],
        compiler_params=_cparams(("parallel", "arbitrary")),
        name="mlp",
    )(x, w_up, w_down, ln_g.reshape(1, D), ln_b.reshape(1, D))


def _cum_kernel(x_ref, bias_ref, c0_ref, lf_ref, c_ref, *, apply_ls, tb):
    T = x_ref.shape[2]
    upper = (_iota((tb, tb), 0) <= _iota((tb, tb), 1)).astype(F32)
    carry = c0_ref[0, :, 0:1]
    for t in range(T // tb):
        cols = slice(t * tb, (t + 1) * tb)
        x = x_ref[0, :, cols]
        if apply_ls:
            z = x + bias_ref[:, 0:1]
            x = jnp.minimum(z, 0.0) - jnp.log(1.0 + jnp.exp(-jnp.abs(z)))
        lf_ref[0, :, cols] = x
        cs = mmx(x, upper, exact="b") + carry
        c_ref[0, :, cols] = cs
        carry = cs[:, tb - 1:tb]


def cum_logf(x_row, bias, c0, apply_ls):
    B, R, T = x_row.shape
    row = pl.BlockSpec((1, R, T), lambda b: (b, 0, 0))
    return pl.pallas_call(
        functools.partial(_cum_kernel, apply_ls=apply_ls, tb=min(T, 512)),
        grid=(B,),
        in_specs=[row, _full_spec((R, LANE)), pl.BlockSpec((1, R, LANE), lambda b: (b, 0, 0))],
        out_specs=[row, row],
        out_shape=[jax.ShapeDtypeStruct((B, R, T), F32)] * 2,
        compiler_params=_cparams(("parallel",)),
        name="cum_logf",
    )(x_row, bias, c0)


QK_PIECES = 1

def _q3(q):
    if QK_PIECES == 1:
        return q.astype(BF16)
    qh, ql = _split2(q)
    return jnp.concatenate([qh, ql, qh], axis=1)


def _k3(k, axis=1):
    if QK_PIECES == 1:
        return k.astype(BF16)
    kh, kl = _split2(k)
    return jnp.concatenate([kh, kh, kl], axis=axis)


def _attend(q3, sources, nrows, dv):
    carry = (jnp.full((nrows, 1), NEG, F32), jnp.zeros((nrows, 1), F32), jnp.zeros((nrows, dv), F32))
    for k3_tile, v_tile, bias, mask, j_lo, j_hi in sources:
        def body(j, carry, k3_tile=k3_tile, v_tile=v_tile, bias=bias, mask=mask):
            m, l, acc = carry
            s = _dg(q3, k3_tile(j), nt=True) + bias(j)
            if mask is not None:
                keep = mask(j)
                s = jnp.where(keep, s, NEG)
            m_new = jnp.maximum(m, jnp.max(s, -1, keepdims=True))
            alpha = jnp.exp(m - m_new)
            e = jnp.exp(s - m_new)
            if mask is not None:
                e = jnp.where(keep, e, 0.0)
            l = alpha * l + jnp.sum(e, -1, keepdims=True)
            acc = alpha * acc + _dg(e.astype(BF16), v_tile(j))
            return m_new, l, acc
        if isinstance(j_lo, int) and isinstance(j_hi, int) and j_hi - j_lo == 1:
            carry = body(j_lo, carry)
        else:
            carry = lax.fori_loop(j_lo, j_hi, body, carry)
    m, l, acc = carry
    return acc / jnp.maximum(l, 1e-30)


def _stack_heads_q(q, tq):
    keep = (_iota((NH * tq, 1), 0) // tq) == (_iota((1, WM), 1) // HEAD)
    return jnp.where(keep, jnp.concatenate([q] * NH, axis=0), 0.0)


def _unstack_heads(o, tq):
    lane_head = _iota((1, WM), 1) // HEAD
    out = o[0:tq]
    for h in range(1, NH):
        out = jnp.where(lane_head == h, o[h * tq:(h + 1) * tq], out)
    return out


def _per_head_rows(fn, tq):
    return jnp.concatenate([fn(h, slice(h * tq, (h + 1) * tq)) for h in range(NH)], axis=0)


def _rank_lt(score, n, top):
    lane = _iota(score.shape, 1)
    rank = jnp.zeros(score.shape, F32)
    for s2 in range(n):
        col = score[:, s2:s2 + 1]
        ahead = (col > score) | ((col == score) & (s2 < lane))
        rank = rank + ahead.astype(F32)
    return rank < top


def _kv_tiles(k_ref, v_ref, tk, scratch, first_step):
    n_main = k_ref.shape[0] // tk
    tile = lambda ref, j: ref[pl.ds(pl.multiple_of(j * tk, tk), tk), :]
    if scratch is None:
        return (lambda j: _k3(tile(k_ref, j))), (lambda j: tile(v_ref, j).astype(BF16))
    k3_ref, vb_ref = scratch

    @pl.when(first_step)
    def _():
        for j in range(n_main):
            rows = slice(j * tk, (j + 1) * tk)
            k3_ref[rows, :] = _k3(k_ref[rows, :])
            vb_ref[rows, :] = v_ref[rows, :].astype(BF16)

    return (lambda j: tile(k3_ref, j)), (lambda j: tile(vb_ref, j))


def _fox_kernel(*refs, tq, tk, qoff, new_off, cache_kv):
    has_new = new_off is not None
    refs = list(refs)
    q_ref, cq_ref, k_ref, v_ref, ck_ref = refs[:5]
    kn_ref, vn_ref, ckn_ref = refs[5:8] if has_new else (None, None, None)
    o_ref = refs[8 if has_new else 5]
    scratch = tuple(refs[-2:]) if cache_kv else None
    qb = pl.program_id(1)
    R = NH * tq
    n_main = k_ref.shape[0] // tk
    k3_tile, v_tile = _kv_tiles(k_ref, v_ref, tk, scratch, qb == 0)
    q0 = qoff + qb * tq
    qpos = q0 + _iota((R, 1), 0) % tq
    q3 = _q3(_stack_heads_q(q_ref[...] * (HEAD ** -0.5), tq))
    cq = jnp.concatenate([cq_ref[0, :, h:h + 1] for h in range(NH)], axis=0)

    def bias_main(j):
        ck = ck_ref[0, j]
        return _per_head_rows(lambda h, rows: cq[rows] - ck[h:h + 1, :], tq)

    mask_main = lambda j: (j * tk + _iota((1, tk), 1)) <= qpos
    j_full = jnp.minimum(n_main, q0 // tk)
    j_hi = jnp.minimum(n_main, (q0 + tq - 1) // tk + 1)
    srcs = [(k3_tile, v_tile, bias_main, None, 0, j_full), (k3_tile, v_tile, bias_main, mask_main, j_full, j_hi)]
    if has_new:
        bias_new = lambda j: _per_head_rows(lambda h, rows: cq[rows] - ckn_ref[0, h:h + 1, :], tq)
        mask_new = lambda j: (new_off + _iota((1, SUB), 1)) <= qpos
        srcs.append((lambda j: _k3(kn_ref[...]), lambda j: vn_ref[...].astype(BF16), bias_new, mask_new, 0, 1))
    o_ref[...] = _unstack_heads(_attend(q3, srcs, R, WM), tq)


def fox_attention(hq, hk, nb, T, cq_col, ck_main, new=None, qoff=0):
    tq = min(T, 128)
    nq = T // tq
    qspec = pl.BlockSpec((tq, WM), lambda b, i: (b * nq + i, C_FQ // WM))
    cqspec = pl.BlockSpec((1, tq, NH), lambda b, i: (b, i, 0))
    tk = ck_main.shape[-1]
    if new is None:
        L = T
        ins = [hq, cq_col, hk, hk, ck_main]
        specs = [qspec, cqspec, pl.BlockSpec((L, WM), lambda b, i: (b, C_FK // WM)),
                 pl.BlockSpec((L, WM), lambda b, i: (b, C_FV // WM)),
                 pl.BlockSpec((1, L // tk, SUB, tk), lambda b, i: (b, 0, 0, 0))]
        new_off = None
    else:
        k_main, v_main, ck_new, new_off = new
        L = k_main.shape[1]
        ins = [hq, cq_col, k_main.reshape(nb * L, WM), v_main.reshape(nb * L, WM), ck_main, hq, hq, ck_new]
        specs = [qspec, cqspec, pl.BlockSpec((L, WM), lambda b, i: (b, 0)), pl.BlockSpec((L, WM), lambda b, i: (b, 0)),
                 pl.BlockSpec((1, L // tk, SUB, tk), lambda b, i: (b, 0, 0, 0)),
                 pl.BlockSpec((tq, WM), lambda b, i: (b * nq + i, C_FK // WM)),
                 pl.BlockSpec((tq, WM), lambda b, i: (b * nq + i, C_FV // WM)),
                 pl.BlockSpec((1, SUB, SUB), lambda b, i: (b, 0, 0))]
        assert tq == SUB and nq == 1
    cache_kv = nq > 1
    scratch = [pltpu.VMEM((L, QK_PIECES * WM), BF16), pltpu.VMEM((L, WM), BF16)] if cache_kv else []
    return pl.pallas_call(
        functools.partial(_fox_kernel, tq=tq, tk=tk, qoff=qoff, new_off=new_off, cache_kv=cache_kv),
        grid=(nb, nq),
        in_specs=specs,
        out_specs=pl.BlockSpec((tq, WM), lambda b, i: (b * nq + i, 0)),
        out_shape=jax.ShapeDtypeStruct((nb * T, WM), F32),
        scratch_shapes=scratch,
        compiler_params=_cparams(("parallel", "arbitrary")),
        name="fox",
    )(*ins)


def _moba_kernel(*refs, tq, tk, qoff, new_off, cache_kv):
    has_new = new_off is not None
    refs = list(refs)
    q_ref, k_ref, v_ref = refs[:3]
    kn_ref, vn_ref = refs[3:5] if has_new else (None, None)
    o_ref, kmean_ref = refs[5:7] if has_new else refs[3:5]
    scratch = tuple(refs[-2:]) if cache_kv else None
    qb = pl.program_id(1)
    R = NH * tq
    L = k_ref.shape[0]
    nblk = L // MOBA_BLOCK
    n_main = L // tk
    blocks_per_tile = tk // MOBA_BLOCK

    @pl.when(qb == 0)
    def _():
        kmean_ref[...] = jnp.zeros_like(kmean_ref)
        for n in range(nblk):
            blk = k_ref[n * MOBA_BLOCK:(n + 1) * MOBA_BLOCK, :]
            kmean_ref[n:n + 1, :] = jnp.sum(blk, axis=0, keepdims=True) * (1.0 / MOBA_BLOCK)

    k3_tile, v_tile = _kv_tiles(k_ref, v_ref, tk, scratch, qb == 0)
    q0 = qoff + qb * tq
    qpos = q0 + _iota((R, 1), 0) % tq
    cur = qpos // MOBA_BLOCK
    qbd = _stack_heads_q(q_ref[...], tq)
    q3 = _q3(qbd * (HEAD ** -0.5))
    row_head = _iota((R, 1), 0) // tq
    slope = jnp.zeros((R, 1), F32)
    for h in range(NH):
        slope = jnp.where(row_head == h, SLOPES_MOBA[h], slope)
    col_q = -slope * qpos.astype(F32)

    blk_lane = _iota((1, LANE), 1)
    gs = mm(qbd, kmean_ref[...], nt=True)
    gs = jnp.where((blk_lane < cur) & (blk_lane < nblk), gs, NEG)
    sel = _rank_lt(gs, nblk, MOBA_TOP) & (gs > 0.5 * NEG)
    sel_bias = jnp.where(sel, 0.0, NEG)

    def alibi(kpos, col):
        kposf = kpos.astype(F32)
        return _per_head_rows(lambda h, rows: SLOPES_MOBA[h] * kposf + col[rows], tq)

    def bias_main(j):
        parts = []
        for i in range(blocks_per_tile):
            b = j * blocks_per_tile + i
            col_sel = jnp.sum(jnp.where(blk_lane == b, sel_bias, 0.0), axis=1, keepdims=True)
            parts.append(alibi(b * MOBA_BLOCK + _iota((1, MOBA_BLOCK), 1), col_sel + col_q))
        return parts[0] if blocks_per_tile == 1 else jnp.concatenate(parts, axis=1)

    srcs = [(k3_tile, v_tile, bias_main, None, 0, jnp.minimum(n_main, q0 // tk))]
    if has_new:
        kpos_own = new_off + _iota((1, SUB), 1)
        own = (lambda j: _k3(kn_ref[...]), lambda j: vn_ref[...].astype(BF16))
    else:
        assert blocks_per_tile == 1 and MOBA_BLOCK % tq == 0
        j_own = q0 // tk
        kpos_own = j_own * tk + _iota((1, tk), 1)
        own = (lambda j: k3_tile(j_own), lambda j: v_tile(j_own))
    mask_own = lambda j: ((kpos_own // MOBA_BLOCK) == cur) & (kpos_own <= qpos)
    srcs.append((own[0], own[1], lambda j: alibi(kpos_own, col_q), mask_own, 0, 1))
    o_ref[...] = _unstack_heads(_attend(q3, srcs, R, WM), tq)


def moba_attention(hq, nb, T, new=None, qoff=0):
    tq = min(T, 128)
    nq = T // tq
    qspec = pl.BlockSpec((tq, WM), lambda b, i: (b * nq + i, C_MQ // WM))
    if new is None:
        L = T
        ins = [hq, hq, hq]
        specs = [qspec, pl.BlockSpec((L, WM), lambda b, i: (b, C_MK // WM)),
                 pl.BlockSpec((L, WM), lambda b, i: (b, C_MV // WM))]
        new_off = None
    else:
        k_main, v_main, new_off = new
        L = k_main.shape[1]
        ins = [hq, k_main.reshape(nb * L, WM), v_main.reshape(nb * L, WM), hq, hq]
        specs = [qspec, pl.BlockSpec((L, WM), lambda b, i: (b, 0)), pl.BlockSpec((L, WM), lambda b, i: (b, 0)),
                 pl.BlockSpec((tq, WM), lambda b, i: (b * nq + i, C_MK // WM)),
                 pl.BlockSpec((tq, WM), lambda b, i: (b * nq + i, C_MV // WM))]
        assert tq == SUB and nq == 1
    tk = MOBA_BLOCK if new is None else min(L, 4 * MOBA_BLOCK)
    assert L // MOBA_BLOCK <= LANE
    cache_kv = nq > 1
    scratch = [pltpu.VMEM((L, QK_PIECES * WM), BF16), pltpu.VMEM((L, WM), BF16)] if cache_kv else []
    return pl.pallas_call(
        functools.partial(_moba_kernel, tq=tq, tk=tk, qoff=qoff, new_off=new_off, cache_kv=cache_kv),
        grid=(nb, nq),
        in_specs=specs,
        out_specs=pl.BlockSpec((tq, WM), lambda b, i: (b * nq + i, 0)),
        out_shape=jax.ShapeDtypeStruct((nb * T, WM), F32),
        scratch_shapes=[pltpu.VMEM((LANE, WM), F32)] + scratch,
        compiler_params=_cparams(("parallel", "arbitrary")),
        name="moba",
    )(*ins)


def _compress_kernel(g_ref, pos_ref, w1_ref, w2_ref, o_ref):
    half = g_ref.shape[1]
    ng = g_ref.shape[0]
    g = g_ref[...]
    a = mm(g + pos_ref[0:1, :], w1_ref[0:half, :])
    b = mm(g + pos_ref[1:2, :], w1_ref[half:2 * half, :])
    shift = (_iota((ng, ng), 1) == _iota((ng, ng), 0) + 1).astype(F32)
    pre = a + mmx(shift, b, exact="a")
    act = pre * _sigmoid(pre)
    o_ref[...] = mm(act, w2_ref[...])


def nsa_compress(rows_pair, nb, pos_pair, w1_pair, w2_pair):
    L = rows_pair.shape[0] // nb
    ng = L // CMP_STRIDE
    half = CMP_STRIDE * LANE
    g = rows_pair.reshape(nb * ng, half)
    return pl.pallas_call(
        _compress_kernel,
        grid=(nb,),
        in_specs=[pl.BlockSpec((ng, half), lambda b: (b, 0)), _full_spec(pos_pair.shape), _full_spec(w1_pair.shape),
                  _full_spec(w2_pair.shape)],
        out_specs=pl.BlockSpec((ng, LANE), lambda b: (b, 0)),
        out_shape=jax.ShapeDtypeStruct((nb * ng, LANE), F32),
        compiler_params=_cparams(("parallel",)),
        name="nsa_compress",
    )(g, pos_pair, w1_pair, w2_pair)


def _nsa_kernel(*refs, tq, tk, qoff, n_sel, sel_srcs, win_srcs, cache_kv):
    n_s, n_w = len(sel_srcs), len(win_srcs)
    q_ref, sm_ref, kc_ref = refs[:3]
    sel_refs = refs[3:3 + n_s]
    win_refs = refs[3 + n_s:3 + n_s + n_w]
    o_ref = refs[3 + n_s + n_w]
    scratch = refs[4 + n_s + n_w:] if cache_kv else (None, None)
    passes = 3
    qb = pl.program_id(1)
    R = NH * tq
    q0 = qoff + qb * tq
    qpos1 = q0 + _iota((tq, 1), 0)
    qpos = jnp.concatenate([qpos1] * NH, axis=0)
    row_head = _iota((R, 1), 0) // tq
    slope = jnp.zeros((R, 1), F32)
    for h in range(NH):
        slope = jnp.where(row_head == h, SLOPES_NSA[h], slope)
    lane = _iota((1, LANE), 1)
    scale = HEAD ** -0.5

    qs = []
    for h in range(NH):
        pair = q_ref[:, (h // 2) * LANE:(h // 2 + 1) * LANE]
        if h % 2:
            pair = pltpu.roll(pair, HEAD, axis=1)
        qs.append(jnp.where(lane < HEAD, pair, 0.0))
    q = jnp.concatenate(qs, axis=0)

    nc = kc_ref.shape[0]
    kcv = kc_ref[...]
    cend = _iota((1, nc), 1) * CMP_STRIDE + (CMP_LEN - 1)
    s_c = mm(q, kcv, nt=True, passes=passes) * scale - slope * (qpos - cend).astype(F32)
    m_c = cend <= qpos
    z = jnp.where(m_c, s_c, NEG)
    e = jnp.where(m_c, jnp.exp(z - jnp.max(z, -1, keepdims=True)), 0.0)
    p_c = e / jnp.maximum(jnp.sum(e, -1, keepdims=True), 1e-30)
    o_c = mm(p_c, kcv, passes=passes)

    n_sel_p = -(-n_sel // LANE) * LANE
    p_sum = p_c[0:tq]
    for h in range(1, NH):
        p_sum = p_sum + p_c[h * tq:(h + 1) * tq]
    cn = _iota((nc, n_sel_p), 0) * CMP_STRIDE
    sb = _iota((nc, n_sel_p), 1)
    overlap = ((cn < (sb + 1) * SEL_BLOCK) & (cn + (CMP_LEN - 1) >= sb * SEL_BLOCK)).astype(F32)
    imp = mmx(p_sum, overlap, exact="b")
    blk = _iota((1, n_sel_p), 1)
    cur = qpos1 // SEL_BLOCK
    valid = blk <= cur
    forced = (blk == 0) | (blk == cur) | (blk == cur - 1)
    score = jnp.where(valid, jnp.where(forced, NSA_FORCE, imp), NEG)
    sel1 = (_rank_lt(score, n_sel, SEL_TOP) & (score > 0.5 * NEG)).astype(F32)

    q3 = _q3(q * scale)
    col_q = -slope * qpos.astype(F32)

    def make_srcs(ref, k3_ref, pos0, nrows, kind):
        t = min(tk, nrows)
        n_t = nrows // t
        if k3_ref is None:
            raw = (lambda j: ref[...]) if n_t == 1 else (lambda j: ref[pl.ds(pl.multiple_of(j * t, t), t), :])
            k3_tile = lambda j: _k3(raw(j))
            v_tile = lambda j: raw(j).astype(BF16)
        else:
            @pl.when(qb == 0)
            def _():
                for j in range(n_t):
                    rows = slice(j * t, (j + 1) * t)
                    k3_ref[rows, :] = _k3(ref[rows, :])

            k3_tile = lambda j: k3_ref[pl.ds(pl.multiple_of(j * t, t), t), :]
            v_tile = lambda j: k3_ref[pl.ds(pl.multiple_of(j * t, t), t), 0:LANE]

        def kpos(j):
            return pos0 + j * t + _iota((1, t), 1)

        def alibi(j):
            kposf = kpos(j).astype(F32)
            return _per_head_rows(lambda h, rows: SLOPES_NSA[h] * kposf + col_q[rows], tq)

        def chosen(j):
            kb = (pos0 + j * t + _iota((n_sel_p, t), 1)) // SEL_BLOCK
            expand = (_iota((n_sel_p, t), 0) == kb).astype(BF16)
            return _dg(sel1.astype(BF16), expand)

        if kind == "win":
            def mask(j):
                dist = qpos - kpos(j)
                return (dist >= 0) & (dist < NSA_WIN) & (kpos(j) >= 0)
            if n_t == 1:
                return [(k3_tile, v_tile, alibi, mask, 0, 1)]
            lo = jnp.maximum(0, (q0 - (NSA_WIN - 1) - pos0) // t)
            hi = jnp.minimum(n_t, (q0 + tq - 1 - pos0) // t + 1)
            return [(k3_tile, v_tile, alibi, mask, lo, hi)]

        def mask(j):
            keep = jnp.concatenate([chosen(j) > 0.5] * NH, axis=0)
            return keep & (kpos(j) <= qpos)

        if n_t == 1:
            return [(k3_tile, v_tile, alibi, mask, 0, 1)]

        def bias_past(j):
            excl = (chosen(j) - 1.0) * (-NEG)
            kposf = kpos(j).astype(F32)
            return _per_head_rows(lambda h, rows: (excl + SLOPES_NSA[h] * kposf) + col_q[rows], tq)

        j_full = jnp.minimum(n_t, jnp.maximum(0, q0 - pos0) // t)
        j_hi = jnp.minimum(n_t, (q0 + tq - 1 - pos0) // t + 1)
        return [(k3_tile, v_tile, bias_past, None, 0, j_full), (k3_tile, v_tile, alibi, mask, j_full, j_hi)]

    sel_sources, win_sources = [], []
    for i, (r, (p0, n)) in enumerate(zip(sel_refs, sel_srcs)):
        sel_sources += make_srcs(r, scratch[0] if i == 0 else None, p0, n, "sel")
    for i, (r, (p0, n)) in enumerate(zip(win_refs, win_srcs)):
        win_sources += make_srcs(r, scratch[1] if i == 0 else None, p0, n, "win")
    o_g = _attend(q3, sel_sources, R, LANE)
    o_w = _attend(q3, win_sources, R, LANE)

    gates = _sigmoid(sm_ref[...])

    def gate_col(i):
        return jnp.concatenate([gates[:, S_NG + 3 * h + i:S_NG + 3 * h + i + 1] for h in range(NH)], axis=0)

    out = gate_col(0) * o_c + gate_col(1) * o_g + gate_col(2) * o_w
    for p in range(NH // 2):
        even = pltpu.roll(out[(2 * p) * tq:(2 * p + 1) * tq], HEAD, axis=1)
        odd = out[(2 * p + 1) * tq:(2 * p + 2) * tq]
        o_ref[:, p * LANE:(p + 1) * LANE] = jnp.where(lane < HEAD, even, odd)


def nsa_attention(hq, nb, T, kcvc, sel_list, win_list, n_sel, qoff=0):
    tq = min(T, 128)
    nq = T // tq
    nc = kcvc.shape[0] // nb
    ins = [hq, hq, kcvc]
    specs = [pl.BlockSpec((tq, WM), lambda b, i: (b * nq + i, C_NQ // WM)),
             pl.BlockSpec((tq, LANE), lambda b, i: (b * nq + i, C_SM // LANE)),
             pl.BlockSpec((nc, LANE), lambda b, i: (b, 0))]
    for arr, col, _, rows in list(sel_list) + list(win_list):
        ins.append(arr)
        specs.append(pl.BlockSpec((rows, LANE), lambda b, i, col=col: (b, col)))
    cache_kv = nq > 1
    kern = functools.partial(
        _nsa_kernel, tq=tq, tk=512 if cache_kv else 1024, qoff=qoff, n_sel=n_sel,
        sel_srcs=tuple((p0, rows) for _, _, p0, rows in sel_list),
        win_srcs=tuple((p0, rows) for _, _, p0, rows in win_list), cache_kv=cache_kv)
    scratch = ([pltpu.VMEM((sel_list[0][3], QK_PIECES * LANE), BF16), pltpu.VMEM((win_list[0][3], QK_PIECES * LANE), BF16)]
               if cache_kv else [])
    return pl.pallas_call(
        kern,
        grid=(nb, nq),
        in_specs=specs,
        out_specs=pl.BlockSpec((tq, WM), lambda b, i: (b * nq + i, 0)),
        out_shape=jax.ShapeDtypeStruct((nb * T, WM), F32),
        scratch_shapes=scratch,
        compiler_params=_cparams(("parallel", "arbitrary")),
        name="nsa",
    )(*ins)


def _gdn_kernel(q_ref, k_ref, v_ref, z_ref, sm_ref, cwq_ref, cwk_ref, cwv_ref, cbq_ref, cbk_ref, cbv_ref,
                s0_ref, par_ref, ng_ref, o_ref, sout_ref, xq_ref, xk_ref, xv_ref, s_ref, *, C, t_valid, passes):
    c = pl.program_id(1)
    R = NH * C
    bd = (_iota((WM, WM), 0) // HEAD) == (_iota((WM, WM), 1) // HEAD)
    bd_f = bd.astype(F32)

    @pl.when(c == 0)
    def _():
        xq_ref[0:SUB, :] = cbq_ref[0]
        xk_ref[0:SUB, :] = cbk_ref[0]
        xv_ref[0:SUB, :] = cbv_ref[0]
        spread = ((_iota((HEAD, WM), 1) % HEAD) == _iota((HEAD, WM), 0)).astype(F32)
        s_ref[...] = jnp.where(bd, mmx(s0_ref[0], spread, exact="b"), 0.0)

    def conv(x_ref, ext_ref, cw_ref):
        ext_ref[SUB:SUB + C, :] = x_ref[...]
        y = cw_ref[3:4, :] * ext_ref[SUB:SUB + C, :]
        for j in range(3):
            y = y + cw_ref[j:j + 1, :] * ext_ref[SUB - 3 + j:SUB - 3 + j + C, :]
        ext_ref[0:SUB, :] = ext_ref[C:C + SUB, :]
        return y * _sigmoid(y)

    qc = conv(q_ref, xq_ref, cwq_ref)
    kc = conv(k_ref, xk_ref, cwk_ref)
    vc = conv(v_ref, xv_ref, cwv_ref)
    qn = qc * lax.rsqrt(mmx(qc * qc, bd_f, exact="b") + 1e-6) * (HEAD ** -0.5)
    kn = kc * lax.rsqrt(mmx(kc * kc, bd_f, exact="b") + 1e-6)

    sm = sm_ref[...]
    tok_ok = (c * C + _iota((C, 1), 0)) < t_valid
    beta_t = jnp.where(tok_ok, _sigmoid(sm), 0.0)
    g_t = jnp.where(tok_ok, -jnp.exp(par_ref[0:1, :]) * _softplus(sm + par_ref[1:2, :]), 0.0)
    lower_c = (_iota((C, C), 1) <= _iota((C, C), 0)).astype(F32)
    gam_t = mmx(lower_c, g_t, exact="a")

    def stack_col(tile, lane0):
        return jnp.concatenate([tile[:, lane0 + h:lane0 + h + 1] for h in range(NH)], axis=0)

    bcol = stack_col(beta_t, S_GB)
    gcol = stack_col(gam_t, S_GA)
    gl_row = gam_t[C - 1:C, :]
    glcol = jnp.concatenate([jnp.broadcast_to(gl_row[:, S_GA + h:S_GA + h + 1], (C, 1)) for h in range(NH)], axis=0)
    gl_state = jnp.concatenate([jnp.broadcast_to(gl_row[:, S_GA + h:S_GA + h + 1], (HEAD, 1)) for h in range(NH)], axis=0)
    pick0 = (_iota((SUB, LANE), 1) == 0).astype(F32)
    grow = mmx(pick0, jnp.broadcast_to(gcol, (R, LANE)), nt=True, exact="a")[0:1, :]

    rr = _iota((R, R), 0)
    cc = _iota((R, R), 1)
    same = (rr // C) == (cc // C)
    lower = same & (cc <= rr)
    dec = jnp.where(lower, jnp.exp(jnp.where(lower, gcol - grow, 0.0)), 0.0)

    lane_head = _iota((1, WM), 1) // HEAD

    def stack_heads(x):
        return jnp.concatenate([jnp.where(lane_head == h, x, 0.0) for h in range(NH)], axis=0)

    kst, qst, vst = stack_heads(kn), stack_heads(qn), stack_heads(vc)
    a_mat = bcol * jnp.where(cc < rr, dec, 0.0) * mm(kst, kst, nt=True, passes=passes)
    t_inv = (rr == cc).astype(F32) - a_mat
    pw = a_mat
    for _ in range(int(math.log2(C)) - 1):
        pw = mm(pw, pw, passes=passes)
        t_inv = t_inv + mm(t_inv, pw, passes=passes)
    egam = jnp.exp(gcol)
    w_m = mm(t_inv, (bcol * egam) * kst, passes=passes)
    u_m = mm(t_inv, bcol * vst, passes=passes)
    p_m = dec * mm(qst, kst, nt=True, passes=passes)
    qg = egam * qst
    kd = jnp.exp(glcol - gcol) * kst

    s = s_ref[...]
    u2 = u_m - mm(w_m, s, passes=passes)
    o_st = mm(qg, s, passes=passes) + mm(p_m, u2, passes=passes)
    s_new = jnp.exp(gl_state) * s + mm(kd, u2, tn=True, passes=passes)
    s_ref[...] = s_new

    o = o_st[0:C]
    for h in range(1, NH):
        o = o + o_st[h * C:(h + 1) * C]
    o = o * lax.rsqrt(mmx(o * o, bd_f, exact="b") * (1.0 / HEAD) + 1e-6)
    zt = z_ref[...]
    o_ref[...] = o * ng_ref[...] * (zt * _sigmoid(zt))

    @pl.when(c == pl.num_programs(1) - 1)
    def _():
        gather = ((_iota((WM, HEAD), 0) % HEAD) == _iota((WM, HEAD), 1)).astype(F32)
        sout_ref[0] = mmx(s_new, gather, exact="b")


def gdn_branch(h, nb, T, t_valid, conv_w, conv_buf, s0, a_log, dt_bias, norm_g, passes=3):
    C = min(T, GDN_CHUNK)
    nchunk = T // C
    row = lambda col: pl.BlockSpec((C, WM), lambda b, c, col=col: (b * nchunk + c, col // WM))
    cb = jnp.pad(conv_buf, ((0, 0), (SUB - 3, 0), (0, 0)))
    par = jnp.zeros((SUB, LANE), F32).at[0, S_GA:S_GA + NH].set(a_log).at[1, S_GA:S_GA + NH].set(dt_bias)
    ng = jnp.tile(norm_g, NH).reshape(1, WM)
    cws = [conv_w[:, i * WM:(i + 1) * WM] for i in range(3)]
    cb_spec = lambda i: pl.BlockSpec((1, SUB, WM), lambda b, c, i=i: (b, 0, i))
    o, s_new = pl.pallas_call(
        functools.partial(_gdn_kernel, C=C, t_valid=t_valid, passes=passes),
        grid=(nb, nchunk),
        in_specs=[row(C_GQ), row(C_GK), row(C_GV), row(C_GZ),
                  pl.BlockSpec((C, LANE), lambda b, c: (b * nchunk + c, C_SM // LANE))]
        + [_full_spec((4, WM))] * 3 + [cb_spec(0), cb_spec(1), cb_spec(2),
                                        pl.BlockSpec((1, WM, HEAD), lambda b, c: (b, 0, 0)),
                                        _full_spec((SUB, LANE)), _full_spec((1, WM))],
        out_specs=[pl.BlockSpec((C, WM), lambda b, c: (b * nchunk + c, 0)),
                   pl.BlockSpec((1, WM, HEAD), lambda b, c: (b, 0, 0))],
        out_shape=[jax.ShapeDtypeStruct((nb * T, WM), F32), jax.ShapeDtypeStruct((nb, WM, HEAD), F32)],
        scratch_shapes=[pltpu.VMEM((SUB + C, WM), F32)] * 3 + [pltpu.VMEM((WM, WM), F32)],
        compiler_params=_cparams(("parallel", "arbitrary")),
        name="gdn",
    )(h, h, h, h, h, *cws, cb, cb, cb, s0.reshape(nb, WM, HEAD), par, ng)
    return o, s_new.reshape(nb, NH, HEAD, HEAD)


def key_minor(cache):
    nd = cache.ndim
    t = jnp.transpose(cache, (0, 1) + tuple(range(3, nd)) + (2,))
    return t.reshape(t.shape[0], t.shape[1], -1, t.shape[-1])


def _page_specs(cache_t, layer):
    G = PAGES_PER_STEP
    r = cache_t.shape[2]
    return [pl.BlockSpec((1, 1, r, PAGE), lambda b, i, pt, g=g: (pt[b, i * G + g], layer, 0, 0)) for g in range(G)]


def _paged_call(kern, page_table, layer, caches_t, others, other_specs, out_shape, out_spec, scratch, name):
    nb, n_pages = page_table.shape
    G = PAGES_PER_STEP
    ins, specs = list(others), list(other_specs)
    for c in caches_t:
        ins += [c] * G
        specs += _page_specs(c, layer)
    return pl.pallas_call(
        kern,
        grid_spec=pltpu.PrefetchScalarGridSpec(num_scalar_prefetch=1, grid=(nb, n_pages // G), in_specs=specs,
                                               out_specs=out_spec, scratch_shapes=scratch),
        out_shape=out_shape,
        compiler_params=_cparams(("parallel", "arbitrary")),
        name=name,
    )(page_table, *ins)


def _gather_logf_kernel(pt_ref, *refs):
    del pt_ref
    G = PAGES_PER_STEP
    o_ref = refs[G]
    o_ref[...] = jnp.zeros_like(o_ref)
    for g in range(G):
        o_ref[0, 0:NH, g * PAGE:(g + 1) * PAGE] = refs[g][0, 0]


def gather_logf(cache_t, page_table, layer):
    nb, n_pages = page_table.shape
    G = PAGES_PER_STEP
    return _paged_call(_gather_logf_kernel, page_table, layer, [cache_t], [], [],
                       jax.ShapeDtypeStruct((nb, SUB, n_pages * PAGE), F32),
                       pl.BlockSpec((1, SUB, G * PAGE), lambda b, i, pt: (b, 0, i)), [], "gather_logf")


def _gather_pair_kernel(pt_ref, *refs):
    del pt_ref
    G = PAGES_PER_STEP
    o_ref = refs[2 * G]
    ident = (_iota((PAGE, PAGE), 0) == _iota((PAGE, PAGE), 1)).astype(F32)
    for g in range(G):
        stacked = jnp.concatenate([refs[g][0, 0], refs[G + g][0, 0]], axis=0)
        o_ref[0, g * PAGE:(g + 1) * PAGE, :] = mmx(ident, stacked, nt=True, exact="a")


def gather_pair(cache_a_t, cache_b_t, page_table, layer):
    nb, n_pages = page_table.shape
    G = PAGES_PER_STEP
    return _paged_call(_gather_pair_kernel, page_table, layer, [cache_a_t, cache_b_t], [], [],
                       jax.ShapeDtypeStruct((nb, n_pages * PAGE, LANE), F32),
                       pl.BlockSpec((1, G * PAGE, LANE), lambda b, i, pt: (b, i, 0)), [], "gather_pair")


def _kmean_kernel(pt_ref, *refs):
    del pt_ref
    G = PAGES_PER_STEP
    o_ref = refs[G]
    j = pl.program_id(1)

    @pl.when(j == 0)
    def _():
        o_ref[...] = jnp.zeros_like(o_ref)

    acc = o_ref[0]
    for g in range(G):
        blk = (j * G + g) // (MOBA_BLOCK // PAGE)
        to_block = (_iota((PAGE, LANE), 1) == blk).astype(F32)
        acc = acc + mmx(refs[g][0, 0], to_block, exact="b") * (1.0 / MOBA_BLOCK)
    o_ref[0] = acc


def moba_block_means(cache_k_t, page_table, layer):
    nb, _ = page_table.shape
    return _paged_call(_kmean_kernel, page_table, layer, [cache_k_t], [], [],
                       jax.ShapeDtypeStruct((nb, WM, LANE), F32),
                       pl.BlockSpec((1, WM, LANE), lambda b, i, pt: (b, 0, 0)), [], "moba_block_means")


def _paged_attn_kernel(pt_ref, *refs, kind, tq, new_off):
    del pt_ref
    G = PAGES_PER_STEP
    R = NH * tq
    tk = G * PAGE
    nblk = new_off // MOBA_BLOCK
    n_in = 6 if kind == "fox" else 4
    q_ref, kn_ref, vn_ref = refs[:3]
    kt_refs = refs[n_in:n_in + G]
    vt_refs = refs[n_in + G:n_in + 2 * G]
    o_ref, m_ref, l_ref, acc_ref = refs[n_in + 2 * G:n_in + 2 * G + 4]
    j = pl.program_id(1)
    qpos = new_off + _iota((R, 1), 0) % tq
    kpos_new = new_off + _iota((1, SUB), 1)
    qbd = _stack_heads_q(q_ref[...], tq)
    q3 = _q3(qbd * (HEAD ** -0.5))

    if kind == "fox":
        cq_ref, ck_ref, ckn_ref = refs[3:6]
        cq = jnp.concatenate([cq_ref[0, :, h:h + 1] for h in range(NH)], axis=0)
        ck = ck_ref[0, 0]
        bias = _per_head_rows(lambda h, rows: cq[rows] - ck[h:h + 1, :], tq)
        bias_new = _per_head_rows(lambda h, rows: cq[rows] - ckn_ref[0, h:h + 1, :], tq)
        mask_new = kpos_new <= qpos
    else:
        km_ref = refs[3]
        selb_ref = refs[n_in + 2 * G + 4]
        cur = qpos // MOBA_BLOCK
        row_head = _iota((R, 1), 0) // tq
        slope = jnp.zeros((R, 1), F32)
        for h in range(NH):
            slope = jnp.where(row_head == h, SLOPES_MOBA[h], slope)
        col_q = -slope * qpos.astype(F32)
        blk_lane = _iota((1, LANE), 1)

        @pl.when(j == 0)
        def _():
            gs = mm(qbd, km_ref[0])
            gs = jnp.where((blk_lane < cur) & (blk_lane < nblk), gs, NEG)
            sel = _rank_lt(gs, nblk, MOBA_TOP) & (gs > 0.5 * NEG)
            selb_ref[...] = jnp.where(sel, 0.0, NEG)

        def alibi(kpos, col):
            kposf = kpos.astype(F32)
            return _per_head_rows(lambda h, rows: SLOPES_MOBA[h] * kposf + col[rows], tq)

        parts = []
        for i in range(tk // MOBA_BLOCK):
            b = j * (tk // MOBA_BLOCK) + i
            col_sel = jnp.sum(jnp.where(blk_lane == b, selb_ref[...], 0.0), axis=1, keepdims=True)
            parts.append(alibi(b * MOBA_BLOCK + _iota((1, MOBA_BLOCK), 1), col_sel + col_q))
        bias = jnp.concatenate(parts, axis=1)
        bias_new = alibi(kpos_new, col_q)
        mask_new = ((kpos_new // MOBA_BLOCK) == cur) & (kpos_new <= qpos)

    @pl.when(j == 0)
    def _():
        m_ref[...] = jnp.full(m_ref.shape, NEG, F32)
        l_ref[...] = jnp.zeros_like(l_ref)
        acc_ref[...] = jnp.zeros_like(acc_ref)

    def update(carry, s, pv, keep):
        m, l, acc = carry
        if keep is not None:
            s = jnp.where(keep, s, NEG)
        m_new = jnp.maximum(m, jnp.max(s, -1, keepdims=True))
        alpha = jnp.exp(m - m_new)
        e = jnp.exp(s - m_new)
        if keep is not None:
            e = jnp.where(keep, e, 0.0)
        return m_new, alpha * l + jnp.sum(e, -1, keepdims=True), alpha * acc + pv(e.astype(BF16))

    kt = jnp.concatenate([r[0, 0] for r in kt_refs], axis=1)
    vt = jnp.concatenate([r[0, 0] for r in vt_refs], axis=1).astype(BF16)
    s = _dg(q3, _k3(kt, axis=0)) + bias
    carry = update((m_ref[:, 0:1], l_ref[:, 0:1], acc_ref[...]), s, lambda e: _dg(e, vt, nt=True), None)
    m_ref[...] = jnp.broadcast_to(carry[0], m_ref.shape)
    l_ref[...] = jnp.broadcast_to(carry[1], l_ref.shape)
    acc_ref[...] = carry[2]

    @pl.when(j == pl.num_programs(1) - 1)
    def _():
        vn = vn_ref[...].astype(BF16)
        s_new = _dg(q3, _k3(kn_ref[...]), nt=True) + bias_new
        _, l, acc = update(carry, s_new, lambda e: _dg(e, vn), mask_new)
        o_ref[...] = _unstack_heads(acc / jnp.maximum(l, 1e-30), tq)


def paged_attention(kind, hs, nb, tq, page_table, layer, cache_k_t, cache_v_t, extras, new_off):
    G = PAGES_PER_STEP
    R = NH * tq
    c_q, c_k, c_v = (C_FQ, C_FK, C_FV) if kind == "fox" else (C_MQ, C_MK, C_MV)
    row = lambda col: pl.BlockSpec((tq, WM), lambda b, i, pt, col=col: (b, col // WM))
    others = [hs, hs, hs] + list(extras)
    specs = [row(c_q), row(c_k), row(c_v)]
    if kind == "fox":
        tk = extras[1].shape[-1]
        assert tk == G * PAGE
        specs += [pl.BlockSpec((1, tq, NH), lambda b, i, pt: (b, 0, 0)),
                  pl.BlockSpec((1, 1, SUB, tk), lambda b, i, pt: (b, i, 0, 0)),
                  pl.BlockSpec((1, SUB, SUB), lambda b, i, pt: (b, 0, 0))]
    else:
        specs += [pl.BlockSpec((1, WM, LANE), lambda b, i, pt: (b, 0, 0))]
    scratch = [pltpu.VMEM((R, LANE), F32), pltpu.VMEM((R, LANE), F32), pltpu.VMEM((R, WM), F32)]
    if kind == "moba":
        scratch.append(pltpu.VMEM((R, LANE), F32))
    return _paged_call(functools.partial(_paged_attn_kernel, kind=kind, tq=tq, new_off=new_off), page_table, layer,
                       [cache_k_t, cache_v_t], others, specs, jax.ShapeDtypeStruct((nb * tq, WM), F32),
                       pl.BlockSpec((tq, WM), lambda b, i, pt: (b, 0)), scratch, "paged_" + kind)


def _prep_layer(l, w_in, nsa_cmp_pos_k, nsa_cmp_k_w1, nsa_cmp_k_w2, nsa_cmp_pos_v, nsa_cmp_v_w1, nsa_cmp_v_w2):
    w = w_in[l]
    o = {}
    off = 0
    for name, size in (("gqkv", 3 * WM), ("gz", WM), ("gb", NH), ("ga", NH), ("fqkv", 3 * WM), ("ff", NH), ("nq", WM),
                       ("nkv", 6 * HEAD), ("ng", 3 * NH), ("mqkv", 3 * WM), ("mg", w.shape[1] - 3224)):
        o[name] = w[:, off:off + size]
        off += size
    small = jnp.concatenate([o["gb"], o["ga"], o["ff"], o["ng"]], 1)
    small = jnp.pad(small, ((0, 0), (0, LANE - small.shape[1])))
    w_r = jnp.concatenate([o["mg"], o["gqkv"], o["gz"], o["fqkv"], o["mqkv"], o["nq"], o["nkv"], small], 1)
    z1 = jnp.zeros((CMP_LEN, HEAD, HEAD), F32)
    w1k = nsa_cmp_k_w1[l].reshape(CMP_LEN, HEAD, HEAD)
    w1v = nsa_cmp_v_w1[l].reshape(CMP_LEN, HEAD, HEAD)
    w1_pair = jnp.concatenate([jnp.concatenate([w1k, z1], 2), jnp.concatenate([z1, w1v], 2)], 1).reshape(CMP_LEN * LANE, LANE)
    z2 = jnp.zeros((HEAD, HEAD), F32)
    w2_pair = jnp.concatenate([jnp.concatenate([nsa_cmp_k_w2[l], z2], 1), jnp.concatenate([z2, nsa_cmp_v_w2[l]], 1)], 0)
    pos_pair = jnp.concatenate([nsa_cmp_pos_k[l], nsa_cmp_pos_v[l]], 1).reshape(2, CMP_STRIDE * LANE)
    return w_r, pos_pair, w1_pair, w2_pair


def _rows_to_tiles(c_row, tk):
    nb, r, L = c_row.shape
    return c_row.reshape(nb, r, L // tk, tk).transpose(0, 2, 1, 3)


def kernel(x_prompt, x_sample, state_gdn_s, state_gdn_conv, cache_fox_k, cache_fox_v, cache_fox_logf, cache_nsa_ck, cache_nsa_cv, cache_nsa_sk, cache_nsa_sv, state_nsa_wk, state_nsa_wv, cache_moba_k, cache_moba_v, page_table, w_in, gdn_conv_w, gdn_a_log, gdn_dt_bias, gdn_norm_g, fox_f_bias, nsa_cmp_pos_k, nsa_cmp_k_w1, nsa_cmp_k_w2, nsa_cmp_pos_v, nsa_cmp_v_w1, nsa_cmp_v_w2, w_branch, w_out, ln1_g, ln1_b, w_up, w_down, ln2_g, ln2_b):
    B, T, D = x_prompt.shape
    Bs, Ts, _ = x_sample.shape
    depth = w_in.shape[0]
    n_pages = page_table.shape[1]
    P = n_pages * PAGE
    TP = SUB
    alpha = (2 * depth) ** 0.25
    n_pool = cache_fox_k.shape[0]
    PREC = 3

    xp = x_prompt.reshape(B * T, D)
    xs = jnp.pad(x_sample, ((0, 0), (0, TP - Ts), (0, 0))).reshape(Bs * TP, D)
    fk_t, fv_t, lf_t = key_minor(cache_fox_k), key_minor(cache_fox_v), key_minor(cache_fox_logf)
    mk_t, mv_t = key_minor(cache_moba_k), key_minor(cache_moba_v)
    nck_t, ncv_t = key_minor(cache_nsa_ck), key_minor(cache_nsa_cv)
    nsk_t, nsv_t = key_minor(cache_nsa_sk), key_minor(cache_nsa_sv)
    zeros_c0 = lambda nb: jnp.zeros((nb, SUB, LANE), F32)
    pad_heads = lambda a: jnp.pad(a, ((0, 0), (0, SUB - NH), (0, 0)))

    names = ("gdn_s", "gdn_conv", "fox_k", "fox_v", "fox_logf", "nsa_ck", "nsa_cv", "nsa_sk", "nsa_sv",
             "nsa_wk", "nsa_wv", "moba_k", "moba_v")
    acc_p = {n: [] for n in names}
    acc_s = {n: [] for n in names}

    for l in range(depth):
        w_r, pos_pair, w1_pair, w2_pair = _prep_layer(l, w_in, nsa_cmp_pos_k, nsa_cmp_k_w1, nsa_cmp_k_w2,
                                                      nsa_cmp_pos_v, nsa_cmp_v_w1, nsa_cmp_v_w2)
        f_bias = jnp.broadcast_to(jnp.pad(fox_f_bias[l], (0, SUB - NH))[:, None], (SUB, LANE))
        wb_bf, wo_bf = w_branch[l].astype(BF16), w_out[l].astype(BF16)
        wu_bf, wd_bf = w_up[l].astype(BF16), w_down[l].astype(BF16)

        hp = project(xp, w_r)
        h3 = hp.reshape(B, T, N_PROJ)
        o_a, s_new = gdn_branch(hp, B, T, T, gdn_conv_w[l], jnp.zeros((B, 3, 3 * WM), F32),
                                jnp.zeros((B, NH, HEAD, HEAD), F32), gdn_a_log[l], gdn_dt_bias[l], gdn_norm_g[l], PREC)
        f_row = pad_heads(jnp.swapaxes(h3[:, :, C_SM + S_FF:C_SM + S_FF + NH], 1, 2))
        logf_row, c_row = cum_logf(f_row, f_bias, zeros_c0(B), True)
        tk = min(T, 512)
        o_b = fox_attention(hp, hp, B, T, jnp.swapaxes(c_row[:, :NH], 1, 2), _rows_to_tiles(c_row, tk))
        nsa_c = h3[:, :, C_NC:C_NC + LANE].reshape(B * T, LANE)
        kcvc = nsa_compress(nsa_c, B, pos_pair, w1_pair, w2_pair)
        o_c = nsa_attention(hp, B, T, kcvc, [(hp, C_NS // LANE, 0, T)], [(hp, C_NW // LANE, 0, T)],
                            n_sel=-(-T // SEL_BLOCK))
        o_d = moba_attention(hp, B, T)
        x1 = mix_layer(xp, (o_a, o_b, o_c, o_d), hp, wb_bf, wo_bf, ln1_g[l], ln1_b[l], alpha)
        xp = mlp_layer(x1, wu_bf, wd_bf, ln2_g[l], ln2_b[l], alpha)

        keep = min(NSA_WIN, T)
        acc_p["gdn_s"].append(s_new)
        acc_p["gdn_conv"].append(h3[:, T - 3:, C_GQ:C_GQ + 3 * WM])
        acc_p["fox_k"].append(h3[:, :, C_FK:C_FK + WM].reshape(B, T, NH, HEAD))
        acc_p["fox_v"].append(h3[:, :, C_FV:C_FV + WM].reshape(B, T, NH, HEAD))
        acc_p["fox_logf"].append(jnp.swapaxes(logf_row[:, :NH], 1, 2))
        for i, n in enumerate(("nsa_ck", "nsa_cv", "nsa_sk", "nsa_sv")):
            acc_p[n].append(h3[:, :, C_NC + i * HEAD:C_NC + (i + 1) * HEAD])
        acc_p["nsa_wk"].append(h3[:, T - keep:, C_NW:C_NW + HEAD])
        acc_p["nsa_wv"].append(h3[:, T - keep:, C_NW + HEAD:C_NW + 2 * HEAD])
        acc_p["moba_k"].append(h3[:, :, C_MK:C_MK + WM].reshape(B, T, NH, HEAD))
        acc_p["moba_v"].append(h3[:, :, C_MV:C_MV + WM].reshape(B, T, NH, HEAD))

        hs = project(xs, w_r)
        h3s = hs.reshape(Bs, TP, N_PROJ)
        o_a, s_new = gdn_branch(hs, Bs, TP, Ts, gdn_conv_w[l], state_gdn_conv[:, l], state_gdn_s[:, l],
                                gdn_a_log[l], gdn_dt_bias[l], gdn_norm_g[l], PREC)
        _, c_past = cum_logf(gather_logf(lf_t, page_table, l), f_bias, zeros_c0(Bs), False)
        f_new = pad_heads(jnp.swapaxes(h3s[:, :, C_SM + S_FF:C_SM + S_FF + NH], 1, 2))
        f_new = jnp.pad(f_new, ((0, 0), (0, 0), (0, LANE - TP)))
        c0 = jnp.broadcast_to(c_past[:, :, P - 1:P], (Bs, SUB, LANE))
        logf_new, c_new = cum_logf(f_new, f_bias, c0, True)
        o_b = paged_attention("fox", hs, Bs, TP, page_table, l, fk_t, fv_t,
                              (jnp.swapaxes(c_new[:, :NH, :TP], 1, 2), _rows_to_tiles(c_past, PAGES_PER_STEP * PAGE),
                               c_new[:, :, :TP]), P)
        past_c = gather_pair(nck_t, ncv_t, page_table, l).reshape(Bs * P, LANE)
        past_s = gather_pair(nsk_t, nsv_t, page_table, l).reshape(Bs * P, LANE)
        kcvc = nsa_compress(past_c, Bs, pos_pair, w1_pair, w2_pair)
        n_buf = state_nsa_wk.shape[2]
        wbuf = jnp.concatenate([state_nsa_wk[:, l], state_nsa_wv[:, l]], -1).reshape(Bs * n_buf, LANE)
        o_c = nsa_attention(hs, Bs, TP, kcvc, [(past_s, 0, 0, P), (hs, C_NS // LANE, P, TP)],
                            [(wbuf, 0, P - n_buf, n_buf), (hs, C_NW // LANE, P, TP)],
                            n_sel=-(-(P + Ts) // SEL_BLOCK), qoff=P)
        o_d = paged_attention("moba", hs, Bs, TP, page_table, l, mk_t, mv_t,
                              (moba_block_means(mk_t, page_table, l),), P)
        x1 = mix_layer(xs, (o_a, o_b, o_c, o_d), hs, wb_bf, wo_bf, ln1_g[l], ln1_b[l], alpha)
        xs = mlp_layer(x1, wu_bf, wd_bf, ln2_g[l], ln2_b[l], alpha)

        acc_s["gdn_s"].append(s_new)
        acc_s["gdn_conv"].append(h3s[:, Ts - 3:Ts, C_GQ:C_GQ + 3 * WM])
        acc_s["fox_k"].append(h3s[:, :Ts, C_FK:C_FK + WM].reshape(Bs, Ts, NH, HEAD))
        acc_s["fox_v"].append(h3s[:, :Ts, C_FV:C_FV + WM].reshape(Bs, Ts, NH, HEAD))
        acc_s["fox_logf"].append(jnp.swapaxes(logf_new[:, :NH, :Ts], 1, 2))
        for i, n in enumerate(("nsa_ck", "nsa_cv", "nsa_sk", "nsa_sv")):
            acc_s[n].append(h3s[:, :Ts, C_NC + i * HEAD:C_NC + (i + 1) * HEAD])
        kw = jnp.concatenate([state_nsa_wk[:, l], h3s[:, :Ts, C_NW:C_NW + HEAD]], 1)
        vw = jnp.concatenate([state_nsa_wv[:, l], h3s[:, :Ts, C_NW + HEAD:C_NW + 2 * HEAD]], 1)
        keep = min(NSA_WIN, kw.shape[1])
        acc_s["nsa_wk"].append(kw[:, kw.shape[1] - keep:])
        acc_s["nsa_wv"].append(vw[:, vw.shape[1] - keep:])
        acc_s["moba_k"].append(h3s[:, :Ts, C_MK:C_MK + WM].reshape(Bs, Ts, NH, HEAD))
        acc_s["moba_v"].append(h3s[:, :Ts, C_MV:C_MV + WM].reshape(Bs, Ts, NH, HEAD))

    outs = [xp.reshape(B, T, D), xs.reshape(Bs, TP, D)[:, :Ts]]
    for n in names:
        outs.append(jnp.stack(acc_p[n], 1))
        outs.append(jnp.stack(acc_s[n], 1))
    return tuple(outs)
```
